```python
import math
import jax, jax.numpy as jnp
from jax import lax
import numpy as np

D_MODEL = 1024
BATCH = 2
SEQ = 8192
DEPTH = 1
DEC_BATCH = 32
DEC_SEQ = 4
PAST_LEN = 8192
PAGE_SIZE = 128

FOX_HEADS = 8
FOX_HEAD_DIM = 64
FOX_WIDTH = FOX_HEADS * FOX_HEAD_DIM
FORGET_BIAS_CENTER = 3.0
Q_BLOCK = 128
GDN_HEADS = 4
GDN_KEY_DIM = 128
GDN_VAL_DIM = 128
GDN_QK_WIDTH = GDN_HEADS * GDN_KEY_DIM
GDN_V_WIDTH = GDN_HEADS * GDN_VAL_DIM
GDN_CONV_WIDTH = 4
GDN_CONV_CH = 2 * GDN_QK_WIDTH + GDN_V_WIDTH
GDN_CHUNK = 64
D_FF = 4 * D_MODEL
PLE_DIM = 256
NORM_EPS = 1e-6
IN_SPLITS = (FOX_WIDTH, FOX_WIDTH, FOX_WIDTH, FOX_HEADS, GDN_CONV_CH, GDN_HEADS, GDN_HEADS, GDN_V_WIDTH, 2 * D_MODEL)
D_IN = 3 * FOX_WIDTH + FOX_HEADS + GDN_CONV_CH + 2 * GDN_HEADS + GDN_V_WIDTH + 2 * D_MODEL

kernel_name = "fox_gdn_parallel_hybrid_step"


def rmsnorm(x, g):
    xf = x.astype(jnp.float32)
    y = xf * lax.rsqrt(jnp.mean(xf * xf, axis=-1, keepdims=True) + NORM_EPS)
    return (y * g.astype(jnp.float32)).astype(x.dtype)


def l2norm(x):
    xf = x.astype(jnp.float32)
    return xf * lax.rsqrt(jnp.sum(xf * xf, axis=-1, keepdims=True) + NORM_EPS)


def split_columns(proj):
    outs = []
    start = 0
    for width in IN_SPLITS:
        outs.append(proj[..., start:start + width])
        start += width
    return outs


def fox_prompt_attend(q, k, v, logf):
    b, seq_len, h, dh = q.shape
    n_blocks = seq_len // Q_BLOCK
    cum = jnp.cumsum(logf, axis=1).transpose(0, 2, 1)
    q_blocks = jnp.moveaxis(q.reshape(b, n_blocks, Q_BLOCK, h, dh), 1, 0)
    cum_blocks = jnp.moveaxis(cum.reshape(b, h, n_blocks, Q_BLOCK), 2, 0)
    key_pos = jnp.arange(seq_len)
    scale = dh ** -0.5

    def one_block(args):
        blk, q_i, cum_i = args
        s = jnp.einsum('bqhd,bkhd->bhqk', q_i, k).astype(jnp.float32) * scale
        s = s + cum_i[..., :, None] - cum[:, :, None, :]
        q_pos = blk * Q_BLOCK + jnp.arange(Q_BLOCK)
        s = jnp.where(q_pos[:, None] >= key_pos[None, :], s, -jnp.inf)
        p = jax.nn.softmax(s, axis=-1).astype(v.dtype)
        return jnp.einsum('bhqk,bkhd->bqhd', p, v)

    out = lax.map(one_block, (jnp.arange(n_blocks), q_blocks, cum_blocks))
    return jnp.moveaxis(out, 0, 1).reshape(b, seq_len, h, dh)


def fox_sample_attend(q, k, v, logf, k_pool, v_pool, logf_pool, page_table):
    b, s_new, h, dh = q.shape
    n_pages = page_table.shape[1]
    past = n_pages * k_pool.shape[1]
    k_past = k_pool[page_table].reshape(b, past, h, dh)
    v_past = v_pool[page_table].reshape(b, past, h, dh)
    lf_past = logf_pool[page_table].reshape(b, past, h)
    k_all = jnp.concatenate([k_past.astype(k.dtype), k], axis=1)
    v_all = jnp.concatenate([v_past.astype(v.dtype), v], axis=1)
    cum = jnp.cumsum(jnp.concatenate([lf_past.astype(jnp.float32), logf], axis=1), axis=1).transpose(0, 2, 1)
    s = jnp.einsum('bqhd,bkhd->bhqk', q, k_all).astype(jnp.float32) * (dh ** -0.5)
    s = s + cum[:, :, past:, None] - cum[:, :, None, :]
    q_pos = past + jnp.arange(s_new)
    key_pos = jnp.arange(past + s_new)
    s = jnp.where(q_pos[:, None] >= key_pos[None, :], s, -jnp.inf)
    p = jax.nn.softmax(s, axis=-1).astype(v.dtype)
    return jnp.einsum('bhqk,bkhd->bqhd', p, v_all)


def causal_short_conv(x, buf, w):
    seq_len = x.shape[1]
    xp = jnp.concatenate([buf.astype(x.dtype), x], axis=1)
    y = w[0] * xp[:, 0:seq_len]
    for i in range(1, GDN_CONV_WIDTH):
        y = y + w[i] * xp[:, i:i + seq_len]
    return jax.nn.silu(y), xp[:, seq_len:]


def gated_delta_chunked(q, k, v, g, beta, s0):
    b, seq_len, h, dk = q.shape
    dv = v.shape[-1]
    c = math.gcd(seq_len, GDN_CHUNK)
    n = seq_len // c

    def to_chunks(a):
        a = a.reshape((b, n, c, h) + a.shape[3:])
        return jnp.moveaxis(a, 3, 1)

    qc, kc, vc, gc, bc = (to_chunks(a) for a in (q, k, v, g, beta))
    cum = jnp.cumsum(gc, axis=-1)
    idx = jnp.arange(c)
    causal = idx[:, None] >= idx[None, :]
    strict = idx[:, None] > idx[None, :]
    decay = jnp.exp(jnp.where(causal, cum[..., :, None] - cum[..., None, :], -jnp.inf))
    k_beta = kc * bc[..., None]
    amat = jnp.where(strict, jnp.einsum('bhnid,bhnjd->bhnij', k_beta, kc) * decay, 0.0)
    eye = jnp.eye(c, dtype=amat.dtype)
    tmat = lax.linalg.triangular_solve(amat + eye, jnp.broadcast_to(eye, amat.shape), left_side=True, lower=True)
    u = tmat @ (vc * bc[..., None])
    w = tmat @ (k_beta * jnp.exp(cum)[..., None])
    qk = jnp.einsum('bhnid,bhnjd->bhnij', qc, kc) * decay
    q_dec = qc * jnp.exp(cum)[..., None]
    k_dec = kc * jnp.exp(cum[..., -1:] - cum)[..., None]
    chunk_decay = jnp.exp(cum[..., -1])

    def step(state, xs):
        u_n, w_n, qk_n, q_n, k_n, d_n = xs
        v_new = u_n - w_n @ state
        o_n = q_n @ state + qk_n @ v_new
        state = state * d_n[..., None, None] + jnp.einsum('bhck,bhcv->bhkv', k_n, v_new)
        return state, o_n

    xs = tuple(jnp.moveaxis(a, 2, 0) for a in (u, w, qk, q_dec, k_dec, chunk_decay))
    s_final, o = lax.scan(step, s0, xs)
    o = jnp.moveaxis(jnp.moveaxis(o, 0, 2), 1, 3).reshape(b, seq_len, h, dv)
    return o, s_final


def hybrid_layer(x, ple, lw, attend_fn, conv_buf, ssm_state):
    b, seq_len, _ = x.shape
    f32 = jnp.float32
    xn = rmsnorm(x, lw['norm_mix_g'])
    proj = xn @ lw['w_in']
    fq, fk, fv, ff, gqkv, ga, gb, gz, gates = split_columns(proj)
    fq = rmsnorm(fq.reshape(b, seq_len, FOX_HEADS, FOX_HEAD_DIM), lw['fox_q_norm_g'])
    fk = rmsnorm(fk.reshape(b, seq_len, FOX_HEADS, FOX_HEAD_DIM), lw['fox_k_norm_g'])
    fv = fv.reshape(b, seq_len, FOX_HEADS, FOX_HEAD_DIM)
    logf = jax.nn.log_sigmoid((ff + lw['fox_f_bias']).astype(f32))
    o_a = attend_fn(fq, fk, fv, logf).reshape(b, seq_len, FOX_WIDTH)
    conv_out, conv_new = causal_short_conv(gqkv, conv_buf, lw['gdn_conv_w'])
    gq = conv_out[..., :GDN_QK_WIDTH].reshape(b, seq_len, GDN_HEADS, GDN_KEY_DIM)
    gk = conv_out[..., GDN_QK_WIDTH:2 * GDN_QK_WIDTH].reshape(b, seq_len, GDN_HEADS, GDN_KEY_DIM)
    gv = conv_out[..., 2 * GDN_QK_WIDTH:].reshape(b, seq_len, GDN_HEADS, GDN_VAL_DIM).astype(f32)
    gq = l2norm(gq) * (GDN_KEY_DIM ** -0.5)
    gk = l2norm(gk)
    log_decay = -jnp.exp(lw['gdn_a_log'].astype(f32)) * jax.nn.softplus(ga.astype(f32) + lw['gdn_dt_bias'].astype(f32))
    beta = jax.nn.sigmoid(gb.astype(f32))
    o_b, ssm_new = gated_delta_chunked(gq, gk, gv, log_decay, beta, ssm_state.astype(f32))
    o_b = rmsnorm(o_b, lw['gdn_out_norm_g']) * jax.nn.silu(gz.reshape(b, seq_len, GDN_HEADS, GDN_VAL_DIM).astype(f32))
    o_b = o_b.reshape(b, seq_len, GDN_V_WIDTH).astype(x.dtype)
    gate_a, gate_b = jnp.split(jax.nn.sigmoid(gates), 2, axis=-1)
    merged = gate_a * (o_a @ lw['w_branch_a']) + gate_b * (o_b @ lw['w_branch_b'])
    x = x + merged @ lw['w_out']
    h = rmsnorm(x, lw['norm_mlp_g'])
    x = x + jnp.square(jax.nn.relu(h @ lw['w_up'])) @ lw['w_down']
    ple_gate = jax.nn.sigmoid(rmsnorm(x, lw['norm_ple_g']) @ lw['w_ple_gate'])
    x = x + ple_gate * (ple @ lw['w_ple'])
    return x, (fk, fv, logf, conv_new, ssm_new)


def setup_inputs(seed: int = 0) -> dict:
    key = jax.random.key(seed)
    ks = jax.random.split(key, 32)
    f32 = jnp.float32
    n_pages = PAST_LEN // PAGE_SIZE
    n_used = DEC_BATCH * n_pages
    n_pool = n_used + (n_used + 3) // 4

    def nrm(k, shape, scale):
        return jax.random.normal(k, shape, f32) * scale

    dt = jnp.exp(jax.random.uniform(ks[16], (DEPTH, GDN_HEADS), f32, math.log(1e-3), math.log(1e-1)))
    return {
        'x_prompt': nrm(ks[0], (BATCH, SEQ, D_MODEL), 1.0),
        'x_sample': nrm(ks[1], (DEC_BATCH, DEC_SEQ, D_MODEL), 1.0),
        'p_prompt': nrm(ks[2], (DEPTH, BATCH, SEQ, PLE_DIM), 1.0),
        'p_sample': nrm(ks[3], (DEPTH, DEC_BATCH, DEC_SEQ, PLE_DIM), 1.0),
        'cache_k': nrm(ks[4], (DEPTH, n_pool, PAGE_SIZE, FOX_HEADS, FOX_HEAD_DIM), 1.0),
        'cache_v': nrm(ks[5], (DEPTH, n_pool, PAGE_SIZE, FOX_HEADS, FOX_HEAD_DIM), 1.0),
        'cache_logf': jax.nn.log_sigmoid(FORGET_BIAS_CENTER + nrm(ks[6], (DEPTH, n_pool, PAGE_SIZE, FOX_HEADS), 1.0)),
        'state_conv': nrm(ks[7], (DEPTH, DEC_BATCH, GDN_CONV_WIDTH - 1, GDN_CONV_CH), 1.0),
        'state_ssm': nrm(ks[8], (DEPTH, DEC_BATCH, GDN_HEADS, GDN_KEY_DIM, GDN_VAL_DIM), 0.3),
        'page_table': jax.random.permutation(ks[9], n_pool)[:n_used].reshape(DEC_BATCH, n_pages).astype(jnp.int32),
        'norm_mix_g': 1.0 + nrm(ks[10], (DEPTH, D_MODEL), 0.02),
        'w_in': nrm(ks[11], (DEPTH, D_MODEL, D_IN), D_MODEL ** -0.5),
        'fox_f_bias': FORGET_BIAS_CENTER + nrm(ks[12], (DEPTH, FOX_HEADS), 0.5),
        'fox_q_norm_g': 1.0 + nrm(ks[13], (DEPTH, FOX_HEAD_DIM), 0.02),
        'fox_k_norm_g': 1.0 + nrm(ks[14], (DEPTH, FOX_HEAD_DIM), 0.02),
        'gdn_conv_w': nrm(ks[15], (DEPTH, GDN_CONV_WIDTH, GDN_CONV_CH), GDN_CONV_WIDTH ** -0.5),
        'gdn_a_log': jnp.log(jax.random.uniform(ks[17], (DEPTH, GDN_HEADS), f32, 1.0, 16.0)),
        'gdn_dt_bias': dt + jnp.log(-jnp.expm1(-dt)),
        'gdn_out_norm_g': 1.0 + nrm(ks[18], (DEPTH, GDN_VAL_DIM), 0.02),
        'w_branch_a': nrm(ks[19], (DEPTH, FOX_WIDTH, D_MODEL), FOX_WIDTH ** -0.5),
        'w_branch_b': nrm(ks[20], (DEPTH, GDN_V_WIDTH, D_MODEL), GDN_V_WIDTH ** -0.5),
        'w_out': nrm(ks[21], (DEPTH, D_MODEL, D_MODEL), D_MODEL ** -0.5),
        'norm_mlp_g': 1.0 + nrm(ks[22], (DEPTH, D_MODEL), 0.02),
        'w_up': nrm(ks[23], (DEPTH, D_MODEL, D_FF), D_MODEL ** -0.5),
        'w_down': nrm(ks[24], (DEPTH, D_FF, D_MODEL), D_FF ** -0.5),
        'norm_ple_g': 1.0 + nrm(ks[25], (DEPTH, D_MODEL), 0.02),
        'w_ple_gate': nrm(ks[26], (DEPTH, D_MODEL, D_MODEL), D_MODEL ** -0.5),
        'w_ple': nrm(ks[27], (DEPTH, PLE_DIM, D_MODEL), PLE_DIM ** -0.5),
    }


def reference(x_prompt, x_sample, p_prompt, p_sample, cache_k, cache_v, cache_logf, state_conv, state_ssm, page_table,
              norm_mix_g, w_in, fox_f_bias, fox_q_norm_g, fox_k_norm_g, gdn_conv_w, gdn_a_log, gdn_dt_bias,
              gdn_out_norm_g, w_branch_a, w_branch_b, w_out, norm_mlp_g, w_up, w_down, norm_ple_g, w_ple_gate, w_ple):
    y_prompt = x_prompt
    y_sample = x_sample
    prompt_states = []
    sample_states = []
    b_prompt = x_prompt.shape[0]
    for l in range(DEPTH):
        lw = dict(norm_mix_g=norm_mix_g[l], w_in=w_in[l], fox_f_bias=fox_f_bias[l], fox_q_norm_g=fox_q_norm_g[l],
                  fox_k_norm_g=fox_k_norm_g[l], gdn_conv_w=gdn_conv_w[l], gdn_a_log=gdn_a_log[l],
                  gdn_dt_bias=gdn_dt_bias[l], gdn_out_norm_g=gdn_out_norm_g[l], w_branch_a=w_branch_a[l],
                  w_branch_b=w_branch_b[l], w_out=w_out[l], norm_mlp_g=norm_mlp_g[l], w_up=w_up[l],
                  w_down=w_down[l], norm_ple_g=norm_ple_g[l], w_ple_gate=w_ple_gate[l], w_ple=w_ple[l])
        conv0 = jnp.zeros((b_prompt, GDN_CONV_WIDTH - 1, GDN_CONV_CH), x_prompt.dtype)
        ssm0 = jnp.zeros((b_prompt, GDN_HEADS, GDN_KEY_DIM, GDN_VAL_DIM), jnp.float32)
        y_prompt, st_p = hybrid_layer(y_prompt, p_prompt[l], lw, fox_prompt_attend, conv0, ssm0)

        def sample_attend(q, k, v, logf, l=l):
            return fox_sample_attend(q, k, v, logf, cache_k[l], cache_v[l], cache_logf[l], page_table)

        y_sample, st_s = hybrid_layer(y_sample, p_sample[l], lw, sample_attend, state_conv[l], state_ssm[l])
        prompt_states.append(st_p)
        sample_states.append(st_s)
    k_prompt = jnp.stack([s[0] for s in prompt_states])
    v_prompt = jnp.stack([s[1] for s in prompt_states])
    logf_prompt = jnp.stack([s[2] for s in prompt_states])
    conv_prompt = jnp.stack([s[3] for s in prompt_states])
    ssm_prompt = jnp.stack([s[4] for s in prompt_states])
    k_sample = jnp.stack([s[0] for s in sample_states])
    v_sample = jnp.stack([s[1] for s in sample_states])
    logf_sample = jnp.stack([s[2] for s in sample_states])
    conv_sample = jnp.stack([s[3] for s in sample_states])
    ssm_sample = jnp.stack([s[4] for s in sample_states])
    return (y_prompt, y_sample, k_prompt, v_prompt, logf_prompt, conv_prompt, ssm_prompt,
            k_sample, v_sample, logf_sample, conv_sample, ssm_sample)
```

```python
import functools
import math

import jax
import jax.numpy as jnp
from jax import lax
from jax.experimental import pallas as pl
from jax.experimental.pallas import tpu as pltpu

F32 = jnp.float32
BF16 = jnp.bfloat16

D_MODEL = 1024
FOX_HEADS = 8
FOX_HEAD_DIM = 64
FOX_WIDTH = FOX_HEADS * FOX_HEAD_DIM
GDN_HEADS = 4
GDN_KEY_DIM = 128
GDN_VAL_DIM = 128
GDN_QK_WIDTH = GDN_HEADS * GDN_KEY_DIM
GDN_V_WIDTH = GDN_HEADS * GDN_VAL_DIM
GDN_CONV_WIDTH = 4
GDN_CONV_CH = 2 * GDN_QK_WIDTH + GDN_V_WIDTH
GDN_CHUNK = 64
D_FF = 4 * D_MODEL
PLE_DIM = 256
NORM_EPS = 1e-6

LANES = 128
SUBLANES = 8
NEG_BIG = -1e30
VMEM_LIMIT = 56 * 1024 * 1024

_MAIN_SEGS = (FOX_WIDTH, FOX_WIDTH, FOX_WIDTH, GDN_CONV_CH, GDN_V_WIDTH, 2 * D_MODEL)
_MAIN_OFFS = tuple(sum(_MAIN_SEGS[:i]) for i in range(len(_MAIN_SEGS) + 1))
D_MAIN = _MAIN_OFFS[-1]
ROW_LOGF = 0
ROW_G = FOX_HEADS
ROW_BETA = FOX_HEADS + GDN_HEADS
N_SMALL = FOX_HEADS + 2 * GDN_HEADS


def _dot(a, b):
    return jnp.dot(a, b, preferred_element_type=F32)


def _dot_nt(a, b):
    return lax.dot_general(a, b, (((1,), (1,)), ((), ())), preferred_element_type=F32)


def _dot_tn(a, b):
    return lax.dot_general(a, b, (((0,), (0,)), ((), ())), preferred_element_type=F32)


def _split(a, parts, axis):
    pieces = []
    for _ in range(parts - 1):
        p = a.astype(BF16).astype(F32)
        pieces.append(p)
        a = a - p
    pieces.append(a)
    return jnp.concatenate(pieces, axis=axis).astype(BF16)


def _dot3(a, b):
    m, n = a.shape[0], b.shape[1]
    r = _dot(_split(a, 2, 0), _split(b, 2, 1))
    return (r[:m, :n] + r[:m, n:]) + (r[m:, :n] + r[m:, n:])


def _dot_exact_lhs(a_bf, b):
    n = b.shape[1]
    r = _dot(a_bf, _split(b, 3, 1))
    return r[:, :n] + (r[:, n:2 * n] + r[:, 2 * n:])


def _dot_exact_rhs(a, b_bf):
    m = a.shape[0]
    r = _dot(_split(a, 3, 0), b_bf)
    return r[:m] + (r[m:2 * m] + r[2 * m:])


def _sigmoid(x):
    return 1.0 / (1.0 + jnp.exp(-x))


def _iota(shape, dim):
    return lax.broadcasted_iota(jnp.int32, shape, dim)


def _div_pow2(x, divisor):
    shift = divisor.bit_length() - 1
    assert divisor == 1 << shift
    return lax.shift_right_logical(x, shift)


def _mod_pow2(x, divisor):
    assert divisor & (divisor - 1) == 0
    return x & (divisor - 1)


def _in_proj_kernel(x_ref, gmix_ref, wmain_ref, wsmall_ref, gmat_ref, qg_ref, kg_ref, sbias_ref, alog_ref,
                    q_ref, k_ref, kb_ref, v_ref, vb_ref, gqkv_ref, gz_ref, gates_ref, s_ref, st_ref):
    x = x_ref[...]
    xn = x * lax.rsqrt(jnp.mean(x * x, axis=-1, keepdims=True) + NORM_EPS) * gmix_ref[...]
    xb = xn.astype(BF16)

    def proj(seg):
        return _dot(xb, wmain_ref[:, _MAIN_OFFS[seg]:_MAIN_OFFS[seg + 1]])

    gmat = gmat_ref[...]

    def head_norm(t, g):
        r = _dot(_split(t * t, 2, 0), gmat)
        ms = r[:t.shape[0]] + r[t.shape[0]:]
        return t * lax.rsqrt(ms + NORM_EPS) * g

    q = head_norm(proj(0), qg_ref[...])
    q_ref[...] = (q * (FOX_HEAD_DIM ** -0.5)).astype(BF16)
    k = head_norm(proj(1), kg_ref[...])
    k_ref[...] = k
    kb_ref[...] = k.astype(BF16)
    v = proj(2)
    v_ref[...] = v
    vb_ref[...] = v.astype(BF16)
    gqkv_ref[...] = proj(3)
    gz_ref[...] = proj(4)
    gates_ref[...] = proj(5)

    z = _dot(xb, wsmall_ref[...]).T + sbias_ref[...]
    row = _iota(z.shape, 0)
    t = jnp.log1p(jnp.exp(-jnp.abs(z)))
    logf = jnp.minimum(z, 0.0) - t
    g = -jnp.exp(alog_ref[...]) * (jnp.maximum(z, 0.0) + t)
    beta = _sigmoid(z)
    res = jnp.where(row < ROW_G, logf, jnp.where(row < ROW_BETA, g, jnp.where(row < N_SMALL, beta, 0.0)))
    st_ref[...] = res[:N_SMALL]
    s_ref[...] = res.T


def _in_proj(x, w, tm):
    n = x.shape[0]
    const = lambda i: (0, 0)
    rows = lambda width: pl.BlockSpec((tm, width), lambda i: (i, 0))
    resident = lambda shape: pl.BlockSpec(shape, const, pipeline_mode=pl.Buffered(1))
    out_shape = (
        jax.ShapeDtypeStruct((n, FOX_WIDTH), BF16),
        jax.ShapeDtypeStruct((n, FOX_WIDTH), F32),
        jax.ShapeDtypeStruct((n, FOX_WIDTH), BF16),
        jax.ShapeDtypeStruct((n, FOX_WIDTH), F32),
        jax.ShapeDtypeStruct((n, FOX_WIDTH), BF16),
        jax.ShapeDtypeStruct((n, GDN_CONV_CH), F32),
        jax.ShapeDtypeStruct((n, GDN_V_WIDTH), F32),
        jax.ShapeDtypeStruct((n, 2 * D_MODEL), F32),
        jax.ShapeDtypeStruct((n, LANES), F32),
        jax.ShapeDtypeStruct((N_SMALL, n), F32),
    )
    out_specs = (
        rows(FOX_WIDTH), rows(FOX_WIDTH), rows(FOX_WIDTH), rows(FOX_WIDTH), rows(FOX_WIDTH),
        rows(GDN_CONV_CH), rows(GDN_V_WIDTH), rows(2 * D_MODEL), rows(LANES),
        pl.BlockSpec((N_SMALL, tm), lambda i: (0, i)),
    )
    in_specs = [
        rows(D_MODEL),
        resident((1, D_MODEL)),
        resident((D_MODEL, D_MAIN)),
        resident((D_MODEL, LANES)),
        resident((FOX_WIDTH, FOX_WIDTH)),
        resident((1, FOX_WIDTH)),
        resident((1, FOX_WIDTH)),
        resident((LANES, 1)),
        resident((LANES, 1)),
    ]
    return pl.pallas_call(
        _in_proj_kernel,
        grid=(n // tm,),
        in_specs=in_specs,
        out_specs=out_specs,
        out_shape=out_shape,
        compiler_params=pltpu.CompilerParams(dimension_semantics=("arbitrary",), vmem_limit_bytes=VMEM_LIMIT),
        name="in_proj",
    )(x, w["gmix"], w["w_main"], w["w_small"], w["gmat"], w["qg"], w["kg"], w["sbias"], w["alog"])


def _cumsum_kernel(lf_ref, cum_ref):
    n_chunks = lf_ref.shape[1] // LANES
    upper = (_iota((LANES, LANES), 0) <= _iota((LANES, LANES), 1)).astype(BF16)

    def body(c, carry):
        sl = pl.ds(pl.multiple_of(c * LANES, LANES), LANES)
        cs = _dot_exact_rhs(lf_ref[:, sl], upper) + carry
        cum_ref[:, sl] = cs
        return jnp.broadcast_to(cs[:, LANES - 1:LANES], carry.shape)

    lax.fori_loop(0, n_chunks, body, jnp.zeros((FOX_HEADS, LANES), F32))


def _seq_cumsum(st, seq_len):
    n = st.shape[1]
    spec = pl.BlockSpec((FOX_HEADS, seq_len), lambda b: (0, b))
    return pl.pallas_call(
        _cumsum_kernel,
        grid=(n // seq_len,),
        in_specs=[spec],
        out_specs=spec,
        out_shape=jax.ShapeDtypeStruct((FOX_HEADS, n), F32),
        compiler_params=pltpu.CompilerParams(dimension_semantics=("arbitrary",)),
        name="fox_cumsum",
    )(st)


def _fox_prompt_kernel(q_ref, k_ref, v_ref, cq_ref, ck_ref, o_ref, *, tq):
    qi = pl.program_id(2)
    q2 = q_ref[0]
    lane = _iota((tq, LANES), 1)
    first = lane < FOX_HEAD_DIM
    zero = jnp.zeros_like(q2)
    qs = (jnp.where(first, q2, zero), jnp.where(first, zero, q2))
    cqs = (cq_ref[0, 0], cq_ref[0, 1])
    causal = _iota((tq, tq), 0) >= _iota((tq, tq), 1)

    def step(j, carry, diagonal):
        ms, ls, acc = carry
        sl = pl.ds(pl.multiple_of(j * tq, tq), tq)
        k2 = k_ref[0, sl, :]
        v2 = v_ref[0, sl, :]
        new_ms, new_ls, parts = [], [], []
        for h in range(2):
            s = _dot_nt(qs[h], k2) + (cqs[h] - ck_ref[0, h, :, sl])
            if diagonal:
                s = jnp.where(causal, s, -jnp.inf)
            m_new = jnp.maximum(ms[h], jnp.max(s, axis=-1, keepdims=True))
            alpha = jnp.exp(ms[h] - m_new)
            p = jnp.exp(s - m_new)
            new_ls.append(alpha * ls[h] + jnp.sum(p, axis=-1, keepdims=True))
            new_ms.append(m_new)
            parts.append((alpha, _dot(p.astype(BF16), v2)))
        acc = jnp.where(first, parts[0][0] * acc + parts[0][1], parts[1][0] * acc + parts[1][1])
        return tuple(new_ms), tuple(new_ls), acc

    col = lambda val: jnp.full((tq, 1), val, F32)
    init = ((col(NEG_BIG), col(NEG_BIG)), (col(0.0), col(0.0)), jnp.zeros((tq, LANES), F32))
    carry = lax.fori_loop(0, qi, functools.partial(step, diagonal=False), init)
    ms, ls, acc = step(qi, carry, diagonal=True)
    o_ref[0] = acc * jnp.where(first, 1.0 / ls[0], 1.0 / ls[1])


def _fox_prompt(qb, kb, vb, cq, ck, tq):
    b, seq_len, _ = qb.shape
    pairs = FOX_HEADS // 2
    return pl.pallas_call(
        functools.partial(_fox_prompt_kernel, tq=tq),
        grid=(b, pairs, seq_len // tq),
        in_specs=[
            pl.BlockSpec((1, tq, LANES), lambda b_, p, i: (b_, i, p)),
            pl.BlockSpec((1, seq_len, LANES), lambda b_, p, i: (b_, 0, p)),
            pl.BlockSpec((1, seq_len, LANES), lambda b_, p, i: (b_, 0, p)),
            pl.BlockSpec((1, 2, tq, 1), lambda b_, p, i: (b_, p, i, 0)),
            pl.BlockSpec((1, 2, 1, seq_len), lambda b_, p, i: (b_, p, 0, 0)),
        ],
        out_specs=pl.BlockSpec((1, tq, LANES), lambda b_, p, i: (b_, i, p)),
        out_shape=jax.ShapeDtypeStruct((b, seq_len, FOX_WIDTH), F32),
        compiler_params=pltpu.CompilerParams(dimension_semantics=("arbitrary", "arbitrary", "arbitrary"),
                                             vmem_limit_bytes=VMEM_LIMIT),
        name="fox_prompt_attention",
    )(qb, kb, vb, cq, ck)


def _fox_sample_kernel(pt_ref, q_ref, *refs, pages_per_step, n_new):
    del pt_ref
    pp = pages_per_step
    k_refs, v_refs, lf_refs = refs[:pp], refs[pp:2 * pp], refs[2 * pp:3 * pp]
    knew_ref, vnew_ref, lfnew_ref, o_ref, m_sc, l_sc, acc_sc = refs[3 * pp:]
    step_id = pl.program_id(1)
    n_rows = n_new * FOX_HEADS

    @pl.when(step_id == 0)
    def _():
        m_sc[...] = jnp.full(m_sc.shape, NEG_BIG, F32)
        l_sc[...] = jnp.zeros(l_sc.shape, F32)
        acc_sc[...] = jnp.zeros(acc_sc.shape, F32)

    q = q_ref[0]
    ri = _iota((LANES, LANES), 0)
    ci = _iota((LANES, LANES), 1)
    after = (ri > ci).astype(BF16)
    upto = (ri <= ci).astype(BF16)
    tile_heads = lambda a: jnp.concatenate([a] * n_new, axis=0)

    def update(s, shift, v_bf):
        m_prev = m_sc[...] + shift
        m_new = jnp.maximum(m_prev, jnp.max(s, axis=-1, keepdims=True))
        alpha = jnp.exp(m_prev - m_new)
        p = jnp.exp(s - m_new)
        l_sc[...] = alpha * l_sc[...] + jnp.sum(p, axis=-1, keepdims=True)
        acc_sc[...] = alpha * acc_sc[...] + _dot(p.astype(BF16), v_bf)
        m_sc[...] = m_new

    for j in range(pp):
        lf = lf_refs[j][0]
        suffix = _dot_exact_rhs(lf, after)
        total = suffix[:, 0:1] + lf[:, 0:1]
        s = _dot_nt(q, k_refs[j][0].astype(BF16)) + tile_heads(suffix)
        update(s, tile_heads(total), v_refs[j][0].astype(BF16))

    @pl.when(step_id == pl.num_programs(1) - 1)
    def _():
        cum_rows = tile_heads(_dot_exact_rhs(lfnew_ref[0], upto))
        lane = _iota((n_rows, LANES), 1)
        query = _div_pow2(_iota((n_rows, LANES), 0), FOX_HEADS)
        cum_q = jnp.sum(jnp.where(lane == query, cum_rows, 0.0), axis=-1, keepdims=True)
        s = _dot_nt(q, knew_ref[0]) + (cum_q - cum_rows)
        s = jnp.where(lane <= query, s, -jnp.inf)
        update(s, cum_q, vnew_ref[0])
        out = acc_sc[...] / l_sc[...]
        own = _div_pow2(_iota(out.shape, 1), FOX_HEAD_DIM) == _mod_pow2(_iota(out.shape, 0), FOX_HEADS)
        out = jnp.where(own, out, 0.0)
        o_ref[0] = jnp.concatenate(
            [jnp.sum(out[i * FOX_HEADS:(i + 1) * FOX_HEADS], axis=0, keepdims=True) for i in range(n_new)], axis=0)


def _fox_sample(page_table, q_bd, k_pool, v_pool, lf_pool_t, k_new, v_new, lf_new_t, pages_per_step):
    n_seq, n_pages = page_table.shape
    n_rows = q_bd.shape[1]
    n_new = n_rows // FOX_HEADS
    page = k_pool.shape[1]
    pp = pages_per_step
    steps = n_pages // pp

    def paged(shape, j):
        return pl.BlockSpec(shape, lambda b, s, pt: (pt[b * n_pages + s * pp + j], 0, 0))

    per_seq = lambda shape: pl.BlockSpec(shape, lambda b, s, pt: (b, 0, 0))
    in_specs = ([per_seq((1, n_rows, FOX_WIDTH))]
                + [paged((1, page, FOX_WIDTH), j) for j in range(pp)]
                + [paged((1, page, FOX_WIDTH), j) for j in range(pp)]
                + [paged((1, FOX_HEADS, page), j) for j in range(pp)]
                + [per_seq((1, LANES, FOX_WIDTH)), per_seq((1, LANES, FOX_WIDTH)), per_seq((1, FOX_HEADS, LANES))])
    grid_spec = pltpu.PrefetchScalarGridSpec(
        num_scalar_prefetch=1,
        grid=(n_seq, steps),
        in_specs=in_specs,
        out_specs=per_seq((1, n_new, FOX_WIDTH)),
        scratch_shapes=[pltpu.VMEM((n_rows, 1), F32), pltpu.VMEM((n_rows, 1), F32),
                        pltpu.VMEM((n_rows, FOX_WIDTH), F32)],
    )
    return pl.pallas_call(
        functools.partial(_fox_sample_kernel, pages_per_step=pp, n_new=n_new),
        grid_spec=grid_spec,
        out_shape=jax.ShapeDtypeStruct((n_seq, n_new, FOX_WIDTH), F32),
        compiler_params=pltpu.CompilerParams(dimension_semantics=("arbitrary", "arbitrary"),
                                             vmem_limit_bytes=VMEM_LIMIT),
        name="fox_sample_attention",
    )(page_table.reshape(-1), q_bd, *([k_pool] * pp), *([v_pool] * pp), *([lf_pool_t] * pp),
      k_new, v_new, lf_new_t)


def _unit_lower_inverse(a, same_block, chunk, base):
    t_len = a.shape[0]
    ri = _iota((t_len, t_len), 0)
    ci = _iota((t_len, t_len), 1)
    eye = (ri == ci).astype(F32)
    size = min(base, chunk)
    in_base = same_block(size)
    n = jnp.where(in_base, -a, 0.0)
    inv = eye + n
    power = 1
    while 2 * power < size:
        n = _dot3(n, n)
        inv = inv + _dot3(inv, n)
        power *= 2
    while size < chunk:
        off = jnp.where(same_block(2 * size) & jnp.logical_not(same_block(size)), a, 0.0)
        inv = inv - _dot3(inv, _dot3(off, inv))
        size *= 2
    return inv


def _short_conv(xbuf, conv_w, n_rows):
    first = SUBLANES - GDN_CONV_WIDTH + 1
    y = conv_w[0:1] * xbuf[first:first + n_rows, :]
    for i in range(1, GDN_CONV_WIDTH):
        y = y + conv_w[i:i + 1] * xbuf[first + i:first + i + n_rows, :]
    return y * _sigmoid(y)


class _ChunkMasks:
    def __init__(self, t_len, chunk):
        self.t_len, self.chunk = t_len, chunk
        ri = _iota((t_len, t_len), 0)
        ci = _iota((t_len, t_len), 1)
        self.same_block = lambda size: _div_pow2(ri, size) == _div_pow2(ci, size)
        in_chunk = self.same_block(chunk)
        self.lower = in_chunk & (ri >= ci)
        self.strict = in_chunk & (ri > ci)


def _chunk_cumsums(sblk, masks):
    chunk = masks.chunk
    cum_all = _dot_exact_lhs(masks.lower.astype(BF16), sblk)
    last_all = jnp.concatenate(
        [jnp.broadcast_to(cum_all[(c + 1) * chunk - 1:(c + 1) * chunk, :], (chunk, LANES))
         for c in range(masks.t_len // chunk)], axis=0)
    return cum_all, cum_all.T, last_all


def _delta_rule_chunk_terms(ybuf, sblk, cums, masks, h):
    cum_all, cum_all_t, last_all = cums
    col = lambda a, base: a[:, base + h:base + h + 1]
    beta = col(sblk, ROW_BETA)
    cum = col(cum_all, ROW_G)
    last = col(last_all, ROW_G)
    cum_row = cum_all_t[ROW_G + h:ROW_G + h + 1, :]
    decay = jnp.where(masks.lower, jnp.exp(jnp.where(masks.lower, cum - cum_row, 0.0)), 0.0)

    q = ybuf[:, h * GDN_KEY_DIM:(h + 1) * GDN_KEY_DIM]
    k = ybuf[:, GDN_QK_WIDTH + h * GDN_KEY_DIM:GDN_QK_WIDTH + (h + 1) * GDN_KEY_DIM]
    v = ybuf[:, 2 * GDN_QK_WIDTH + h * GDN_VAL_DIM:2 * GDN_QK_WIDTH + (h + 1) * GDN_VAL_DIM]
    q = q * lax.rsqrt(jnp.sum(q * q, axis=-1, keepdims=True) + NORM_EPS) * (GDN_KEY_DIM ** -0.5)
    k = k * lax.rsqrt(jnp.sum(k * k, axis=-1, keepdims=True) + NORM_EPS)
    k_beta = k * beta
    e_cum = jnp.exp(cum)
    k_bf = k.astype(BF16)

    amat = jnp.where(masks.strict, _dot_nt(k_beta.astype(BF16), k_bf) * decay, 0.0)
    tmat = _unit_lower_inverse(amat, masks.same_block, masks.chunk, base=SUBLANES)
    uw = _dot(tmat.astype(BF16), jnp.concatenate([v * beta, k_beta * e_cum], axis=-1).astype(BF16))
    u, wmat = uw[:, :GDN_VAL_DIM], uw[:, GDN_VAL_DIM:]
    qk = (_dot_nt(q.astype(BF16), k_bf) * decay).astype(BF16)
    q_dec = (q * e_cum).astype(BF16)
    k_dec = (k * jnp.exp(last - cum)).astype(BF16)
    return u, wmat.astype(BF16), qk, q_dec, k_dec, jnp.exp(last)


def _gated_out_norm(o, gz, gnorm):
    o = o * lax.rsqrt(jnp.mean(o * o, axis=-1, keepdims=True) + NORM_EPS) * gnorm
    return o * (gz * _sigmoid(gz))


def _gdn_prompt_kernel(gqkv_ref, s_ref, gz_ref, convw_ref, conv0_ref, s0_ref, gnorm_ref, o_ref, sfin_ref,
                       xbuf, ybuf, state, *, t_len, chunk):
    t = pl.program_id(1)

    @pl.when(t == 0)
    def _():
        xbuf[0:SUBLANES, :] = conv0_ref[0]
        state[...] = s0_ref[0]

    xbuf[SUBLANES:SUBLANES + t_len, :] = gqkv_ref[...]
    ybuf[...] = _short_conv(xbuf, convw_ref[...], t_len)
    xbuf[0:SUBLANES, :] = xbuf[t_len:t_len + SUBLANES, :]

    masks = _ChunkMasks(t_len, chunk)
    sblk = s_ref[...]
    cums = _chunk_cumsums(sblk, masks)
    for h in range(GDN_HEADS):
        u, wmat, qk, q_dec, k_dec, chunk_decay = _delta_rule_chunk_terms(ybuf, sblk, cums, masks, h)
        outs = []
        for c in range(t_len // chunk):
            rows = slice(c * chunk, (c + 1) * chunk)
            st = state[h]
            ws = _dot(jnp.concatenate([wmat[rows], q_dec[rows]], axis=0), st.astype(BF16))
            v_new = u[rows] - ws[:chunk]
            v_new_bf = v_new.astype(BF16)
            outs.append(ws[chunk:] + _dot(qk[rows, rows], v_new_bf))
            state[h] = st * chunk_decay[c * chunk:c * chunk + 1] + _dot_tn(k_dec[rows], v_new_bf)
        o = jnp.concatenate(outs, axis=0)
        head_cols = slice(h * GDN_VAL_DIM, (h + 1) * GDN_VAL_DIM)
        o_ref[:, head_cols] = _gated_out_norm(o, gz_ref[:, head_cols], gnorm_ref[...])

    @pl.when(t == pl.num_programs(1) - 1)
    def _():
        sfin_ref[0] = state[...]


def _gdn_prompt(gqkv, s, gz, conv_w, conv0, s0, gnorm, n_seq, t_len, chunk):
    n = gqkv.shape[0]
    tiles = n // n_seq // t_len
    rows = lambda width: pl.BlockSpec((t_len, width), lambda b, t: (b * tiles + t, 0))
    const = lambda shape: pl.BlockSpec(shape, lambda b, t: (0,) * len(shape))
    state_spec = pl.BlockSpec((1, GDN_HEADS, GDN_KEY_DIM, GDN_VAL_DIM), lambda b, t: (b, 0, 0, 0))
    return pl.pallas_call(
        functools.partial(_gdn_prompt_kernel, t_len=t_len, chunk=chunk),
        grid=(n_seq, tiles),
        in_specs=[rows(GDN_CONV_CH), rows(LANES), rows(GDN_V_WIDTH), const((GDN_CONV_WIDTH, GDN_CONV_CH)),
                  pl.BlockSpec((1, SUBLANES, GDN_CONV_CH), lambda b, t: (b, 0, 0)), state_spec,
                  const((1, GDN_VAL_DIM))],
        out_specs=(rows(GDN_V_WIDTH), state_spec),
        out_shape=(jax.ShapeDtypeStruct((n, GDN_V_WIDTH), F32),
                   jax.ShapeDtypeStruct((n_seq, GDN_HEADS, GDN_KEY_DIM, GDN_VAL_DIM), F32)),
        scratch_shapes=[pltpu.VMEM((t_len + SUBLANES, GDN_CONV_CH), F32), pltpu.VMEM((t_len, GDN_CONV_CH), F32),
                        pltpu.VMEM((GDN_HEADS, GDN_KEY_DIM, GDN_VAL_DIM), F32)],
        compiler_params=pltpu.CompilerParams(dimension_semantics=("arbitrary", "arbitrary"),
                                             vmem_limit_bytes=VMEM_LIMIT),
        name="gated_deltanet_prompt",
    )(gqkv, s, gz, conv_w, conv0, s0, gnorm)


def _gdn_sample_kernel(xin_ref, s_ref, gz_ref, convw_ref, s0_ref, gnorm_ref, o_ref, sfin_ref, xbuf, ybuf,
                       *, n_seq, chunk):
    per_seq = SUBLANES + chunk
    n_in = n_seq * per_seq
    t_len = n_seq * chunk
    xbuf[0:SUBLANES, :] = jnp.zeros((SUBLANES, GDN_CONV_CH), F32)
    xbuf[SUBLANES:SUBLANES + n_in, :] = xin_ref[...]
    y = _short_conv(xbuf, convw_ref[...], n_in)
    for i in range(n_seq):
        ybuf[i * chunk:(i + 1) * chunk, :] = y[i * per_seq + SUBLANES:(i + 1) * per_seq]

    masks = _ChunkMasks(t_len, chunk)
    sblk = s_ref[...]
    cums = _chunk_cumsums(sblk, masks)
    seq_of_row = _div_pow2(_iota((t_len, GDN_KEY_DIM), 0), chunk)
    for h in range(GDN_HEADS):
        u, wmat, qk, q_dec, k_dec, chunk_decay = _delta_rule_chunk_terms(ybuf, sblk, cums, masks, h)
        v_news, reads = [], []
        for i in range(n_seq):
            rows = slice(i * chunk, (i + 1) * chunk)
            ws = _dot(jnp.concatenate([wmat[rows], q_dec[rows]], axis=0), s0_ref[i, h].astype(BF16))
            v_news.append(u[rows] - ws[:chunk])
            reads.append(ws[chunk:])
        v_new_bf = jnp.concatenate(v_news, axis=0).astype(BF16)
        o = jnp.concatenate(reads, axis=0) + _dot(qk, v_new_bf)
        for i in range(n_seq):
            own_rows = jnp.where(seq_of_row == i, k_dec, jnp.zeros_like(k_dec))
            sfin_ref[i, h] = (s0_ref[i, h] * chunk_decay[i * chunk:i * chunk + 1]
                              + _dot_tn(own_rows, v_new_bf))
        head_cols = slice(h * GDN_VAL_DIM, (h + 1) * GDN_VAL_DIM)
        o_ref[:, head_cols] = _gated_out_norm(o, gz_ref[:, head_cols], gnorm_ref[...])


def _gdn_sample(xin, s, gz, conv_w, s0, gnorm, n_seq_total, seq_per_step, chunk):
    steps = n_seq_total // seq_per_step
    per_seq = SUBLANES + chunk
    t_len = seq_per_step * chunk
    rows = lambda width: pl.BlockSpec((t_len, width), lambda i: (i, 0))
    const = lambda shape: pl.BlockSpec(shape, lambda i: (0,) * len(shape))
    state_spec = pl.BlockSpec((seq_per_step, GDN_HEADS, GDN_KEY_DIM, GDN_VAL_DIM), lambda i: (i, 0, 0, 0))
    return pl.pallas_call(
        functools.partial(_gdn_sample_kernel, n_seq=seq_per_step, chunk=chunk),
        grid=(steps,),
        in_specs=[pl.BlockSpec((seq_per_step * per_seq, GDN_CONV_CH), lambda i: (i, 0)), rows(LANES),
                  rows(GDN_V_WIDTH), const((GDN_CONV_WIDTH, GDN_CONV_CH)), state_spec, const((1, GDN_VAL_DIM))],
        out_specs=(rows(GDN_V_WIDTH), state_spec),
        out_shape=(jax.ShapeDtypeStruct((n_seq_total * chunk, GDN_V_WIDTH), F32),
                   jax.ShapeDtypeStruct((n_seq_total, GDN_HEADS, GDN_KEY_DIM, GDN_VAL_DIM), F32)),
        scratch_shapes=[pltpu.VMEM((seq_per_step * per_seq + SUBLANES, GDN_CONV_CH), F32),
                        pltpu.VMEM((t_len, GDN_CONV_CH), F32)],
        compiler_params=pltpu.CompilerParams(dimension_semantics=("arbitrary",), vmem_limit_bytes=VMEM_LIMIT),
        name="gated_deltanet_sample",
    )(xin, s, gz, conv_w, s0, gnorm)


def _post_kernel(x_ref, oa_ref, ob_ref, gates_ref, ple_ref, wa_ref, wb_ref, wout_ref, gmlp_ref, wup_ref,
                 wdown_ref, gple_ref, wpg_ref, wple_ref, y_ref):
    def rms(a, g):
        return a * lax.rsqrt(jnp.mean(a * a, axis=-1, keepdims=True) + NORM_EPS) * g

    gates = _sigmoid(gates_ref[...])
    merged = (gates[:, :D_MODEL] * _dot(oa_ref[...].astype(BF16), wa_ref[...])
              + gates[:, D_MODEL:] * _dot(ob_ref[...].astype(BF16), wb_ref[...]))
    x = x_ref[...] + _dot(merged.astype(BF16), wout_ref[...])
    h = rms(x, gmlp_ref[...]).astype(BF16)
    up = jnp.maximum(_dot(h, wup_ref[...]), 0.0)
    x = x + _dot((up * up).astype(BF16), wdown_ref[...])
    ple_gate = _sigmoid(_dot(rms(x, gple_ref[...]).astype(BF16), wpg_ref[...]))
    y_ref[...] = x + ple_gate * _dot(ple_ref[...].astype(BF16), wple_ref[...])


def _post(x, oa, ob, gates, ple, w, tm):
    n = x.shape[0]
    rows = lambda width: pl.BlockSpec((tm, width), lambda i: (i, 0))
    resident = lambda shape: pl.BlockSpec(shape, lambda i: (0, 0), pipeline_mode=pl.Buffered(1))
    return pl.pallas_call(
        _post_kernel,
        grid=(n // tm,),
        in_specs=[rows(D_MODEL), rows(FOX_WIDTH), rows(GDN_V_WIDTH), rows(2 * D_MODEL), rows(PLE_DIM),
                  resident((FOX_WIDTH, D_MODEL)), resident((GDN_V_WIDTH, D_MODEL)), resident((D_MODEL, D_MODEL)),
                  resident((1, D_MODEL)), resident((D_MODEL, D_FF)), resident((D_FF, D_MODEL)),
                  resident((1, D_MODEL)), resident((D_MODEL, D_MODEL)), resident((PLE_DIM, D_MODEL))],
        out_specs=rows(D_MODEL),
        out_shape=jax.ShapeDtypeStruct((n, D_MODEL), F32),
        compiler_params=pltpu.CompilerParams(dimension_semantics=("arbitrary",), vmem_limit_bytes=VMEM_LIMIT),
        name="merge_mlp_ple",
    )(x, oa, ob, gates, ple, w["w_a"], w["w_b"], w["w_out"], w["gmlp"], w["w_up"], w["w_down"], w["gple"],
      w["w_pg"], w["w_ple"])


def _prepare_weights(l, norm_mix_g, w_in, fox_f_bias, fox_q_norm_g, fox_k_norm_g, gdn_conv_w, gdn_a_log, gdn_dt_bias,
                     gdn_out_norm_g, w_branch_a, w_branch_b, w_out, norm_mlp_g, w_up, w_down, norm_ple_g,
                     w_ple_gate, w_ple):
    wi = w_in[l]
    o_ff = 3 * FOX_WIDTH
    o_gqkv = o_ff + FOX_HEADS
    o_ga = o_gqkv + GDN_CONV_CH
    o_gb = o_ga + GDN_HEADS
    o_gz = o_gb + GDN_HEADS
    o_gates = o_gz + GDN_V_WIDTH
    w_main = jnp.concatenate([wi[:, :o_ff], wi[:, o_gqkv:o_ga], wi[:, o_gz:]], axis=1).astype(BF16)
    w_small = jnp.concatenate([wi[:, o_ff:o_gqkv], wi[:, o_ga:o_gz]], axis=1)
    w_small = jnp.pad(w_small, ((0, 0), (0, LANES - N_SMALL))).astype(BF16)
    pad_col = lambda parts: jnp.pad(jnp.concatenate(parts), (0, LANES - N_SMALL)).reshape(LANES, 1).astype(F32)
    zeros_h = jnp.zeros((GDN_HEADS,), F32)
    head = jnp.arange(FOX_WIDTH) // FOX_HEAD_DIM
    return dict(
        gmix=norm_mix_g[l].reshape(1, D_MODEL),
        w_main=w_main,
        w_small=w_small,
        gmat=((head[:, None] == head[None, :]).astype(F32) / FOX_HEAD_DIM).astype(BF16),
        qg=jnp.tile(fox_q_norm_g[l], FOX_HEADS).reshape(1, FOX_WIDTH),
        kg=jnp.tile(fox_k_norm_g[l], FOX_HEADS).reshape(1, FOX_WIDTH),
        sbias=pad_col([fox_f_bias[l], gdn_dt_bias[l], zeros_h]),
        alog=pad_col([jnp.zeros((FOX_HEADS,), F32), gdn_a_log[l], zeros_h]),
        conv_w=gdn_conv_w[l],
        gnorm=gdn_out_norm_g[l].reshape(1, GDN_VAL_DIM),
        w_a=w_branch_a[l].astype(BF16),
        w_b=w_branch_b[l].astype(BF16),
        w_out=w_out[l].astype(BF16),
        gmlp=norm_mlp_g[l].reshape(1, D_MODEL),
        w_up=w_up[l].astype(BF16),
        w_down=w_down[l].astype(BF16),
        gple=norm_ple_g[l].reshape(1, D_MODEL),
        w_pg=w_ple_gate[l].astype(BF16),
        w_ple=w_ple[l].astype(BF16),
    )


def _pick_tile(n, target):
    t = min(n, target)
    while n % t:
        t //= 2
    return t


def _prompt_layer(x, ple, w):
    b, seq_len, _ = x.shape
    n = b * seq_len
    tm = _pick_tile(n, 256)
    q_bf, k, k_bf, v, v_bf, gqkv, gz, gates, s, st = _in_proj(x.reshape(n, D_MODEL), w, tm)

    cum = _seq_cumsum(st, seq_len)
    ck = cum.reshape(FOX_HEADS, b, 1, seq_len).transpose(1, 0, 2, 3)
    cq = cum.reshape(FOX_HEADS, b, seq_len, 1).transpose(1, 0, 2, 3)
    tq = _pick_tile(seq_len, 512)
    o_a = _fox_prompt(q_bf.reshape(b, seq_len, FOX_WIDTH), k_bf.reshape(b, seq_len, FOX_WIDTH),
                      v_bf.reshape(b, seq_len, FOX_WIDTH), cq, ck, tq)

    chunk = math.gcd(seq_len, GDN_CHUNK)
    t_len = _pick_tile(seq_len, 2 * chunk)
    conv0 = jnp.zeros((b, SUBLANES, GDN_CONV_CH), F32)
    ssm0 = jnp.zeros((b, GDN_HEADS, GDN_KEY_DIM, GDN_VAL_DIM), F32)
    o_b, ssm = _gdn_prompt(gqkv, s, gz, w["conv_w"], conv0, ssm0, w["gnorm"], b, t_len, chunk)

    y = _post(x.reshape(n, D_MODEL), o_a.reshape(n, FOX_WIDTH), o_b, gates, ple.reshape(n, PLE_DIM), w, tm)
    keep = GDN_CONV_WIDTH - 1
    states = (k.reshape(b, seq_len, FOX_HEADS, FOX_HEAD_DIM), v.reshape(b, seq_len, FOX_HEADS, FOX_HEAD_DIM),
              s[:, ROW_LOGF:ROW_LOGF + FOX_HEADS].reshape(b, seq_len, FOX_HEADS),
              gqkv.reshape(b, seq_len, GDN_CONV_CH)[:, seq_len - keep:], ssm)
    return y.reshape(b, seq_len, D_MODEL), states


def _sample_layer(x, ple, w, k_pool, v_pool, lf_pool, conv_buf, ssm_state, page_table):
    b, s_new, _ = x.shape
    n = b * s_new
    keep = GDN_CONV_WIDTH - 1
    assert s_new >= keep and s_new <= SUBLANES
    tm = _pick_tile(n, 256)
    q_bf, k, k_bf, v, v_bf, gqkv, gz, gates, s, st = _in_proj(x.reshape(n, D_MODEL), w, tm)

    n_pool, page = k_pool.shape[:2]
    q4 = q_bf.reshape(b, s_new, FOX_HEADS, 1, FOX_HEAD_DIM)
    eye = jnp.eye(FOX_HEADS, dtype=BF16).reshape(1, 1, FOX_HEADS, FOX_HEADS, 1)
    q_bd = (q4 * eye).reshape(b, s_new * FOX_HEADS, FOX_WIDTH)
    pad_keys = lambda a: jnp.pad(a.reshape(b, s_new, FOX_WIDTH), ((0, 0), (0, LANES - s_new), (0, 0)))
    lf_new_t = jnp.pad(st[ROW_LOGF:ROW_LOGF + FOX_HEADS].reshape(FOX_HEADS, b, s_new).transpose(1, 0, 2),
                       ((0, 0), (0, 0), (0, LANES - s_new)))
    o_a = _fox_sample(page_table, q_bd, k_pool.reshape(n_pool, page, FOX_WIDTH),
                      v_pool.reshape(n_pool, page, FOX_WIDTH), lf_pool.transpose(0, 2, 1),
                      pad_keys(k_bf), pad_keys(v_bf), lf_new_t, pages_per_step=4)

    chunk = SUBLANES
    pad_tok = lambda a: jnp.pad(a.reshape(b, s_new, -1), ((0, 0), (0, chunk - s_new), (0, 0)))
    xin = jnp.concatenate([jnp.pad(conv_buf, ((0, 0), (SUBLANES - keep, 0), (0, 0))), pad_tok(gqkv)], axis=1)
    seq_per_step = _pick_tile(b, LANES // chunk)
    o_b, ssm = _gdn_sample(xin.reshape(b * (SUBLANES + chunk), GDN_CONV_CH), pad_tok(s).reshape(b * chunk, LANES),
                           pad_tok(gz).reshape(b * chunk, GDN_V_WIDTH), w["conv_w"], ssm_state, w["gnorm"],
                           b, seq_per_step, chunk)
    o_b = o_b.reshape(b, chunk, GDN_V_WIDTH)[:, :s_new].reshape(n, GDN_V_WIDTH)

    y = _post(x.reshape(n, D_MODEL), o_a.reshape(n, FOX_WIDTH), o_b, gates, ple.reshape(n, PLE_DIM), w, tm)
    states = (k.reshape(b, s_new, FOX_HEADS, FOX_HEAD_DIM), v.reshape(b, s_new, FOX_HEADS, FOX_HEAD_DIM),
              s[:, ROW_LOGF:ROW_LOGF + FOX_HEADS].reshape(b, s_new, FOX_HEADS),
              gqkv.reshape(b, s_new, GDN_CONV_CH)[:, s_new - keep:], ssm)
    return y.reshape(b, s_new, D_MODEL), states


def kernel(x_prompt, x_sample, p_prompt, p_sample, cache_k, cache_v, cache_logf, state_conv, state_ssm, page_table,
           norm_mix_g, w_in, fox_f_bias, fox_q_norm_g, fox_k_norm_g, gdn_conv_w, gdn_a_log, gdn_dt_bias,
           gdn_out_norm_g, w_branch_a, w_branch_b, w_out, norm_mlp_g, w_up, w_down, norm_ple_g, w_ple_gate, w_ple):
    depth = w_in.shape[0]
    y_prompt, y_sample = x_prompt, x_sample
    prompt_states, sample_states = [], []
    for l in range(depth):
        w = _prepare_weights(l, norm_mix_g, w_in, fox_f_bias, fox_q_norm_g, fox_k_norm_g, gdn_conv_w, gdn_a_log,
                             gdn_dt_bias, gdn_out_norm_g, w_branch_a, w_branch_b, w_out, norm_mlp_g, w_up, w_down,
                             norm_ple_g, w_ple_gate, w_ple)
        y_prompt, st_p = _prompt_layer(y_prompt, p_prompt[l], w)
        y_sample, st_s = _sample_layer(y_sample, p_sample[l], w, cache_k[l], cache_v[l], cache_logf[l],
                                       state_conv[l], state_ssm[l], page_table)
        prompt_states.append(st_p)
        sample_states.append(st_s)
    stack = lambda states, i: jnp.stack([st[i] for st in states])
    return ((y_prompt, y_sample) + tuple(stack(prompt_states, i) for i in range(5))
            + tuple(stack(sample_states, i) for i in range(5)))
```

```python
import functools
import math

import jax
import jax.numpy as jnp
from jax import lax
from jax.experimental import pallas as pl
from jax.experimental.pallas import tpu as pltpu

F32 = jnp.float32
BF16 = jnp.bfloat16

D_MODEL = 1024
FOX_HEADS = 8
FOX_HEAD_DIM = 64
FOX_WIDTH = FOX_HEADS * FOX_HEAD_DIM
GDN_HEADS = 4
GDN_KEY_DIM = 128
GDN_VAL_DIM = 128
GDN_QK_WIDTH = GDN_HEADS * GDN_KEY_DIM
GDN_V_WIDTH = GDN_HEADS * GDN_VAL_DIM
GDN_CONV_WIDTH = 4
GDN_CONV_CH = 2 * GDN_QK_WIDTH + GDN_V_WIDTH
GDN_CHUNK = 64
D_FF = 4 * D_MODEL
PLE_DIM = 256
NORM_EPS = 1e-6

LANES = 128
SUBLANES = 8
NEG_BIG = -1e30
VMEM_LIMIT = 56 * 1024 * 1024

_MAIN_SEGS = (FOX_WIDTH, FOX_WIDTH, FOX_WIDTH, GDN_CONV_CH, GDN_V_WIDTH, 2 * D_MODEL)
_MAIN_OFFS = tuple(sum(_MAIN_SEGS[:i]) for i in range(len(_MAIN_SEGS) + 1))
D_MAIN = _MAIN_OFFS[-1]
ROW_LOGF = 0
ROW_G = FOX_HEADS
ROW_BETA = FOX_HEADS + GDN_HEADS
N_SMALL = FOX_HEADS + 2 * GDN_HEADS


def _dot(a, b):
    return jnp.dot(a, b, preferred_element_type=F32)


def _dot_nt(a, b):
    return lax.dot_general(a, b, (((1,), (1,)), ((), ())), preferred_element_type=F32)


def _dot_tn(a, b):
    return lax.dot_general(a, b, (((0,), (0,)), ((), ())), preferred_element_type=F32)


def _split(a, parts, axis):
    pieces = []
    for _ in range(parts - 1):
        p = a.astype(BF16).astype(F32)
        pieces.append(p)
        a = a - p
    pieces.append(a)
    return jnp.concatenate(pieces, axis=axis).astype(BF16)


def _dot3(a, b):
    m, n = a.shape[0], b.shape[1]
    r = _dot(_split(a, 2, 0), _split(b, 2, 1))
    return (r[:m, :n] + r[:m, n:]) + (r[m:, :n] + r[m:, n:])


def _dot_exact_lhs(a_bf, b):
    n = b.shape[1]
    r = _dot(a_bf, _split(b, 3, 1))
    return r[:, :n] + (r[:, n:2 * n] + r[:, 2 * n:])


def _dot_exact_rhs(a, b_bf):
    m = a.shape[0]
    r = _dot(_split(a, 3, 0), b_bf)
    return r[:m] + (r[m:2 * m] + r[2 * m:])


def _sigmoid(x):
    return 1.0 / (1.0 + jnp.exp(-x))


def _iota(shape, dim):
    return lax.broadcasted_iota(jnp.int32, shape, dim)


def _div_pow2(x, divisor):
    shift = divisor.bit_length() - 1
    assert divisor == 1 << shift
    return lax.shift_right_logical(x, shift)


def _mod_pow2(x, divisor):
    assert divisor & (divisor - 1) == 0
    return x & (divisor - 1)


def _in_proj_kernel(x_ref, gmix_ref, wmain_ref, wsmall_ref, gmat_ref, qg_ref, kg_ref, sbias_ref, alog_ref,
                    q_ref, k_ref, kb_ref, v_ref, vb_ref, gqkv_ref, gz_ref, gates_ref, s_ref, st_ref,
                    *, feature_major_kv):
    x = x_ref[...]
    xn = x * lax.rsqrt(jnp.mean(x * x, axis=-1, keepdims=True) + NORM_EPS) * gmix_ref[...]
    xb = xn.astype(BF16)

    def proj(seg):
        return _dot(xb, wmain_ref[:, _MAIN_OFFS[seg]:_MAIN_OFFS[seg + 1]])

    gmat = gmat_ref[...]

    def head_norm(t, g):
        r = _dot(_split(t * t, 2, 0), gmat)
        ms = r[:t.shape[0]] + r[t.shape[0]:]
        return t * lax.rsqrt(ms + NORM_EPS) * g

    q = head_norm(proj(0), qg_ref[...])
    q_ref[...] = (q * (FOX_HEAD_DIM ** -0.5)).astype(BF16)
    k = head_norm(proj(1), kg_ref[...])
    kb_ref[...] = k.astype(BF16)
    v = proj(2)
    vb_ref[...] = v.astype(BF16)
    if feature_major_kv:
        k_ref[0] = k.T
        v_ref[0] = v.T
    else:
        k_ref[...] = k
        v_ref[...] = v
    gqkv_ref[...] = proj(3)
    gz_ref[...] = proj(4)
    gates_ref[...] = proj(5)

    z = _dot(xb, wsmall_ref[...]).T + sbias_ref[...]
    row = _iota(z.shape, 0)
    t = jnp.log1p(jnp.exp(-jnp.abs(z)))
    logf = jnp.minimum(z, 0.0) - t
    g = -jnp.exp(alog_ref[...]) * (jnp.maximum(z, 0.0) + t)
    beta = _sigmoid(z)
    res = jnp.where(row < ROW_G, logf, jnp.where(row < ROW_BETA, g, jnp.where(row < N_SMALL, beta, 0.0)))
    st_ref[...] = res[:N_SMALL]
    s_ref[...] = res.T


def _in_proj(x, w, tm, kv_seq_len=None):
    n = x.shape[0]
    const = lambda i: (0, 0)
    rows = lambda width: pl.BlockSpec((tm, width), lambda i: (i, 0))
    resident = lambda shape: pl.BlockSpec(shape, const, pipeline_mode=pl.Buffered(1))
    if kv_seq_len is None:
        kv_shape, kv_spec = jax.ShapeDtypeStruct((n, FOX_WIDTH), F32), rows(FOX_WIDTH)
    else:
        tiles = kv_seq_len // tm
        kv_shape = jax.ShapeDtypeStruct((n // kv_seq_len, FOX_WIDTH, kv_seq_len), F32)
        kv_spec = pl.BlockSpec((1, FOX_WIDTH, tm), lambda i: (i // tiles, 0, i % tiles))
    out_shape = (
        jax.ShapeDtypeStruct((n, FOX_WIDTH), BF16),
        kv_shape,
        jax.ShapeDtypeStruct((n, FOX_WIDTH), BF16),
        kv_shape,
        jax.ShapeDtypeStruct((n, FOX_WIDTH), BF16),
        jax.ShapeDtypeStruct((n, GDN_CONV_CH), F32),
        jax.ShapeDtypeStruct((n, GDN_V_WIDTH), F32),
        jax.ShapeDtypeStruct((n, 2 * D_MODEL), F32),
        jax.ShapeDtypeStruct((n, LANES), F32),
        jax.ShapeDtypeStruct((N_SMALL, n), F32),
    )
    out_specs = (
        rows(FOX_WIDTH), kv_spec, rows(FOX_WIDTH), kv_spec, rows(FOX_WIDTH),
        rows(GDN_CONV_CH), rows(GDN_V_WIDTH), rows(2 * D_MODEL), rows(LANES),
        pl.BlockSpec((N_SMALL, tm), lambda i: (0, i)),
    )
    in_specs = [
        rows(D_MODEL),
        resident((1, D_MODEL)),
        resident((D_MODEL, D_MAIN)),
        resident((D_MODEL, LANES)),
        resident((FOX_WIDTH, FOX_WIDTH)),
        resident((1, FOX_WIDTH)),
        resident((1, FOX_WIDTH)),
        resident((LANES, 1)),
        resident((LANES, 1)),
    ]
    return pl.pallas_call(
        functools.partial(_in_proj_kernel, feature_major_kv=kv_seq_len is not None),
        grid=(n // tm,),
        in_specs=in_specs,
        out_specs=out_specs,
        out_shape=out_shape,
        compiler_params=pltpu.CompilerParams(dimension_semantics=("arbitrary",), vmem_limit_bytes=VMEM_LIMIT),
        name="in_proj",
    )(x, w["gmix"], w["w_main"], w["w_small"], w["gmat"], w["qg"], w["kg"], w["sbias"], w["alog"])


def _cumsum_kernel(lf_ref, cum_ref):
    n_chunks = lf_ref.shape[1] // LANES
    upper = (_iota((LANES, LANES), 0) <= _iota((LANES, LANES), 1)).astype(BF16)

    def body(c, carry):
        sl = pl.ds(pl.multiple_of(c * LANES, LANES), LANES)
        cs = carry + _dot_exact_rhs(lf_ref[:, sl], upper)
        cum_ref[:, sl] = cs
        return jnp.broadcast_to(cs[:, LANES - 1:LANES], carry.shape)

    lax.fori_loop(0, n_chunks, body, jnp.zeros((FOX_HEADS, LANES), F32))


def _seq_cumsum(st, seq_len):
    n = st.shape[1]
    spec = pl.BlockSpec((FOX_HEADS, seq_len), lambda b: (0, b))
    return pl.pallas_call(
        _cumsum_kernel,
        grid=(n // seq_len,),
        in_specs=[spec],
        out_specs=spec,
        out_shape=jax.ShapeDtypeStruct((FOX_HEADS, n), F32),
        compiler_params=pltpu.CompilerParams(dimension_semantics=("arbitrary",)),
        name="fox_cumsum",
    )(st)


def _fox_prompt_kernel(q_ref, k_ref, v_ref, cq_ref, ck_ref, o_ref, *, tq):
    qi = pl.program_id(2)
    q2 = q_ref[0]
    lane = _iota((tq, LANES), 1)
    first = lane < FOX_HEAD_DIM
    zero = jnp.zeros_like(q2)
    qs = (jnp.where(first, q2, zero), jnp.where(first, zero, q2))
    cqs = (cq_ref[0, 0], cq_ref[0, 1])
    causal = _iota((tq, tq), 0) >= _iota((tq, tq), 1)

    def step(j, carry, diagonal):
        ms, ls, acc = carry
        sl = pl.ds(pl.multiple_of(j * tq, tq), tq)
        k2 = k_ref[0, sl, :]
        v2 = v_ref[0, sl, :]
        new_ms, new_ls, parts = [], [], []
        for h in range(2):
            s = (cqs[h] - ck_ref[0, h, :, sl]) + _dot_nt(qs[h], k2)
            if diagonal:
                s = jnp.where(causal, s, -jnp.inf)
            m_new = jnp.maximum(ms[h], jnp.max(s, axis=-1, keepdims=True))
            alpha = jnp.exp(ms[h] - m_new)
            p = jnp.exp(s - m_new)
            new_ls.append(alpha * ls[h] + jnp.sum(p, axis=-1, keepdims=True))
            new_ms.append(m_new)
            parts.append((alpha, _dot(p.astype(BF16), v2)))
        acc = jnp.where(first, parts[0][0] * acc + parts[0][1], parts[1][0] * acc + parts[1][1])
        return tuple(new_ms), tuple(new_ls), acc

    col = lambda val: jnp.full((tq, 1), val, F32)
    init = ((col(NEG_BIG), col(NEG_BIG)), (col(0.0), col(0.0)), jnp.zeros((tq, LANES), F32))
    carry = lax.fori_loop(0, qi, functools.partial(step, diagonal=False), init)
    ms, ls, acc = step(qi, carry, diagonal=True)
    o_ref[0] = acc * jnp.where(first, 1.0 / ls[0], 1.0 / ls[1])


def _fox_prompt(qb, kb, vb, cq, ck, tq):
    b, seq_len, _ = qb.shape
    pairs = FOX_HEADS // 2
    return pl.pallas_call(
        functools.partial(_fox_prompt_kernel, tq=tq),
        grid=(b, pairs, seq_len // tq),
        in_specs=[
            pl.BlockSpec((1, tq, LANES), lambda b_, p, i: (b_, i, p)),
            pl.BlockSpec((1, seq_len, LANES), lambda b_, p, i: (b_, 0, p)),
            pl.BlockSpec((1, seq_len, LANES), lambda b_, p, i: (b_, 0, p)),
            pl.BlockSpec((1, 2, tq, 1), lambda b_, p, i: (b_, p, i, 0)),
            pl.BlockSpec((1, 2, 1, seq_len), lambda b_, p, i: (b_, p, 0, 0)),
        ],
        out_specs=pl.BlockSpec((1, tq, LANES), lambda b_, p, i: (b_, i, p)),
        out_shape=jax.ShapeDtypeStruct((b, seq_len, FOX_WIDTH), F32),
        compiler_params=pltpu.CompilerParams(dimension_semantics=("arbitrary", "arbitrary", "arbitrary"),
                                             vmem_limit_bytes=VMEM_LIMIT),
        name="fox_prompt_attention",
    )(qb, kb, vb, cq, ck)


def _fox_sample_kernel(pt_ref, q_ref, *refs, pages_per_step, n_new):
    del pt_ref
    pp = pages_per_step
    k_refs, v_refs, lf_refs = refs[:pp], refs[pp:2 * pp], refs[2 * pp:3 * pp]
    knew_ref, vnew_ref, lfnew_ref, o_ref, m_sc, l_sc, acc_sc = refs[3 * pp:]
    step_id = pl.program_id(1)
    n_rows = n_new * FOX_HEADS

    @pl.when(step_id == 0)
    def _():
        m_sc[...] = jnp.full(m_sc.shape, NEG_BIG, F32)
        l_sc[...] = jnp.zeros(l_sc.shape, F32)
        acc_sc[...] = jnp.zeros(acc_sc.shape, F32)

    q = q_ref[0]
    ri = _iota((LANES, LANES), 0)
    ci = _iota((LANES, LANES), 1)
    after = (ri > ci).astype(BF16)
    upto = (ri <= ci).astype(BF16)
    tile_heads = lambda a: jnp.concatenate([a] * n_new, axis=0)

    def update(s, shift, v_t):
        m_prev = m_sc[...] + shift
        m_new = jnp.maximum(m_prev, jnp.max(s, axis=-1, keepdims=True))
        alpha = jnp.exp(m_prev - m_new)
        p = jnp.exp(s - m_new)
        l_sc[...] = alpha * l_sc[...] + jnp.sum(p, axis=-1, keepdims=True)
        acc_sc[...] = alpha * acc_sc[...] + _dot_nt(p.astype(BF16), v_t)
        m_sc[...] = m_new

    lf = jnp.concatenate([r[0] for r in lf_refs], axis=0)
    suffix = _dot_exact_rhs(lf, after)
    total = suffix[:, 0:1] + lf[:, 0:1]
    bias, later = [], jnp.zeros((FOX_HEADS, 1), F32)
    for j in reversed(range(pp)):
        rows = slice(j * FOX_HEADS, (j + 1) * FOX_HEADS)
        bias.append(tile_heads(suffix[rows] + later))
        later = later + total[rows]
    bias = jnp.concatenate(bias[::-1], axis=1)
    k_t = jnp.concatenate([r[0].astype(BF16) for r in k_refs], axis=1)
    v_t = jnp.concatenate([r[0].astype(BF16) for r in v_refs], axis=1)
    update(bias + _dot(q, k_t), tile_heads(later), v_t)

    @pl.when(step_id == pl.num_programs(1) - 1)
    def _():
        cum_rows = tile_heads(_dot_exact_rhs(lfnew_ref[0], upto))
        lane = _iota((n_rows, LANES), 1)
        query = _div_pow2(_iota((n_rows, LANES), 0), FOX_HEADS)
        cum_q = jnp.sum(jnp.where(lane == query, cum_rows, 0.0), axis=-1, keepdims=True)
        s = (cum_q - cum_rows) + _dot(q, knew_ref[0])
        s = jnp.where(lane <= query, s, -jnp.inf)
        update(s, cum_q, vnew_ref[0])
        out = acc_sc[...] / l_sc[...]
        own = _div_pow2(_iota(out.shape, 1), FOX_HEAD_DIM) == _mod_pow2(_iota(out.shape, 0), FOX_HEADS)
        out = jnp.where(own, out, 0.0)
        o_ref[0] = jnp.concatenate(
            [jnp.sum(out[i * FOX_HEADS:(i + 1) * FOX_HEADS], axis=0, keepdims=True) for i in range(n_new)], axis=0)


def _fox_sample(page_table, q_bd, k_pool_t, v_pool_t, lf_pool_t, k_new_t, v_new_t, lf_new_t, pages_per_step):
    n_seq, n_pages = page_table.shape
    n_rows = q_bd.shape[1]
    n_new = n_rows // FOX_HEADS
    page = k_pool_t.shape[2]
    pp = pages_per_step
    steps = n_pages // pp

    def paged(shape, j):
        return pl.BlockSpec(shape, lambda b, s, pt: (pt[b * n_pages + s * pp + j], 0, 0))

    per_seq = lambda shape: pl.BlockSpec(shape, lambda b, s, pt: (b, 0, 0))
    in_specs = ([per_seq((1, n_rows, FOX_WIDTH))]
                + [paged((1, FOX_WIDTH, page), j) for j in range(pp)]
                + [paged((1, FOX_WIDTH, page), j) for j in range(pp)]
                + [paged((1, FOX_HEADS, page), j) for j in range(pp)]
                + [per_seq((1, FOX_WIDTH, LANES)), per_seq((1, FOX_WIDTH, LANES)), per_seq((1, FOX_HEADS, LANES))])
    grid_spec = pltpu.PrefetchScalarGridSpec(
        num_scalar_prefetch=1,
        grid=(n_seq, steps),
        in_specs=in_specs,
        out_specs=per_seq((1, n_new, FOX_WIDTH)),
        scratch_shapes=[pltpu.VMEM((n_rows, 1), F32), pltpu.VMEM((n_rows, 1), F32),
                        pltpu.VMEM((n_rows, FOX_WIDTH), F32)],
    )
    return pl.pallas_call(
        functools.partial(_fox_sample_kernel, pages_per_step=pp, n_new=n_new),
        grid_spec=grid_spec,
        out_shape=jax.ShapeDtypeStruct((n_seq, n_new, FOX_WIDTH), F32),
        compiler_params=pltpu.CompilerParams(dimension_semantics=("arbitrary", "arbitrary"),
                                             vmem_limit_bytes=VMEM_LIMIT),
        name="fox_sample_attention",
    )(page_table.reshape(-1), q_bd, *([k_pool_t] * pp), *([v_pool_t] * pp), *([lf_pool_t] * pp),
      k_new_t, v_new_t, lf_new_t)


def _unit_lower_inverse(a, same_block, chunk, base):
    t_len = a.shape[0]
    ri = _iota((t_len, t_len), 0)
    ci = _iota((t_len, t_len), 1)
    eye = (ri == ci).astype(F32)
    size = min(base, chunk)
    in_base = same_block(size)
    n = jnp.where(in_base, -a, 0.0)
    inv = eye + n
    power = 1
    while 2 * power < size:
        n = _dot3(n, n)
        inv = inv + _dot3(inv, n)
        power *= 2
    while size < chunk:
        off = jnp.where(same_block(2 * size) & jnp.logical_not(same_block(size)), a, 0.0)
        inv = inv - _dot3(inv, _dot3(off, inv))
        size *= 2
    return inv


def _short_conv(xbuf, conv_w, n_rows):
    first = SUBLANES - GDN_CONV_WIDTH + 1
    y = conv_w[0:1] * xbuf[first:first + n_rows, :]
    for i in range(1, GDN_CONV_WIDTH):
        y = y + conv_w[i:i + 1] * xbuf[first + i:first + i + n_rows, :]
    return y * _sigmoid(y)


class _ChunkMasks:
    def __init__(self, t_len, chunk):
        self.t_len, self.chunk = t_len, chunk
        ri = _iota((t_len, t_len), 0)
        ci = _iota((t_len, t_len), 1)
        self.same_block = lambda size: _div_pow2(ri, size) == _div_pow2(ci, size)
        in_chunk = self.same_block(chunk)
        self.lower = in_chunk & (ri >= ci)
        self.strict = in_chunk & (ri > ci)


def _chunk_cumsums(sblk, masks):
    chunk = masks.chunk
    cum_all = _dot_exact_lhs(masks.lower.astype(BF16), sblk)
    last_all = jnp.concatenate(
        [jnp.broadcast_to(cum_all[(c + 1) * chunk - 1:(c + 1) * chunk, :], (chunk, LANES))
         for c in range(masks.t_len // chunk)], axis=0)
    return cum_all, cum_all.T, last_all


def _delta_rule_chunk_terms(ybuf, sblk, cums, masks, h):
    cum_all, cum_all_t, last_all = cums
    col = lambda a, base: a[:, base + h:base + h + 1]
    beta = col(sblk, ROW_BETA)
    cum = col(cum_all, ROW_G)
    last = col(last_all, ROW_G)
    cum_row = cum_all_t[ROW_G + h:ROW_G + h + 1, :]
    decay = jnp.where(masks.lower, jnp.exp(jnp.where(masks.lower, cum - cum_row, 0.0)), 0.0)

    q = ybuf[:, h * GDN_KEY_DIM:(h + 1) * GDN_KEY_DIM]
    k = ybuf[:, GDN_QK_WIDTH + h * GDN_KEY_DIM:GDN_QK_WIDTH + (h + 1) * GDN_KEY_DIM]
    v = ybuf[:, 2 * GDN_QK_WIDTH + h * GDN_VAL_DIM:2 * GDN_QK_WIDTH + (h + 1) * GDN_VAL_DIM]
    q = q * lax.rsqrt(jnp.sum(q * q, axis=-1, keepdims=True) + NORM_EPS) * (GDN_KEY_DIM ** -0.5)
    k = k * lax.rsqrt(jnp.sum(k * k, axis=-1, keepdims=True) + NORM_EPS)
    k_beta = k * beta
    e_cum = jnp.exp(cum)
    k_bf = k.astype(BF16)

    amat = jnp.where(masks.strict, _dot_nt(k_beta.astype(BF16), k_bf) * decay, 0.0)
    tmat = _unit_lower_inverse(amat, masks.same_block, masks.chunk, base=SUBLANES)
    uw = _dot(tmat.astype(BF16), jnp.concatenate([v * beta, k_beta * e_cum], axis=-1).astype(BF16))
    u, wmat = uw[:, :GDN_VAL_DIM], uw[:, GDN_VAL_DIM:]
    qk = (_dot_nt(q.astype(BF16), k_bf) * decay).astype(BF16)
    q_dec = (q * e_cum).astype(BF16)
    k_dec = (k * jnp.exp(last - cum)).astype(BF16)
    return u, wmat.astype(BF16), qk, q_dec, k_dec, jnp.exp(last)


def _gated_out_norm(o, gz, gnorm):
    o = o * lax.rsqrt(jnp.mean(o * o, axis=-1, keepdims=True) + NORM_EPS) * gnorm
    return o * (gz * _sigmoid(gz))


def _gdn_prompt_kernel(gqkv_ref, s_ref, gz_ref, convw_ref, conv0_ref, s0_ref, gnorm_ref, o_ref, sfin_ref,
                       xbuf, ybuf, state, *, t_len, chunk):
    t = pl.program_id(1)

    @pl.when(t == 0)
    def _():
        xbuf[0:SUBLANES, :] = conv0_ref[0]
        state[...] = s0_ref[0]

    xbuf[SUBLANES:SUBLANES + t_len, :] = gqkv_ref[...]
    ybuf[...] = _short_conv(xbuf, convw_ref[...], t_len)
    xbuf[0:SUBLANES, :] = xbuf[t_len:t_len + SUBLANES, :]

    masks = _ChunkMasks(t_len, chunk)
    sblk = s_ref[...]
    cums = _chunk_cumsums(sblk, masks)
    for h in range(GDN_HEADS):
        u, wmat, qk, q_dec, k_dec, chunk_decay = _delta_rule_chunk_terms(ybuf, sblk, cums, masks, h)
        outs = []
        for c in range(t_len // chunk):
            rows = slice(c * chunk, (c + 1) * chunk)
            st = state[h]
            ws = _dot(jnp.concatenate([wmat[rows], q_dec[rows]], axis=0), st.astype(BF16))
            v_new = u[rows] - ws[:chunk]
            v_new_bf = v_new.astype(BF16)
            outs.append(ws[chunk:] + _dot(qk[rows, rows], v_new_bf))
            state[h] = st * chunk_decay[c * chunk:c * chunk + 1] + _dot_tn(k_dec[rows], v_new_bf)
        o = jnp.concatenate(outs, axis=0)
        head_cols = slice(h * GDN_VAL_DIM, (h + 1) * GDN_VAL_DIM)
        o_ref[:, head_cols] = _gated_out_norm(o, gz_ref[:, head_cols], gnorm_ref[...])

    @pl.when(t == pl.num_programs(1) - 1)
    def _():
        sfin_ref[0] = state[...]


def _gdn_prompt(gqkv, s, gz, conv_w, conv0, s0, gnorm, n_seq, t_len, chunk):
    n = gqkv.shape[0]
    tiles = n // n_seq // t_len
    rows = lambda width: pl.BlockSpec((t_len, width), lambda b, t: (b * tiles + t, 0))
    const = lambda shape: pl.BlockSpec(shape, lambda b, t: (0,) * len(shape))
    state_spec = pl.BlockSpec((1, GDN_HEADS, GDN_KEY_DIM, GDN_VAL_DIM), lambda b, t: (b, 0, 0, 0))
    return pl.pallas_call(
        functools.partial(_gdn_prompt_kernel, t_len=t_len, chunk=chunk),
        grid=(n_seq, tiles),
        in_specs=[rows(GDN_CONV_CH), rows(LANES), rows(GDN_V_WIDTH), const((GDN_CONV_WIDTH, GDN_CONV_CH)),
                  pl.BlockSpec((1, SUBLANES, GDN_CONV_CH), lambda b, t: (b, 0, 0)), state_spec,
                  const((1, GDN_VAL_DIM))],
        out_specs=(rows(GDN_V_WIDTH), state_spec),
        out_shape=(jax.ShapeDtypeStruct((n, GDN_V_WIDTH), F32),
                   jax.ShapeDtypeStruct((n_seq, GDN_HEADS, GDN_KEY_DIM, GDN_VAL_DIM), F32)),
        scratch_shapes=[pltpu.VMEM((t_len + SUBLANES, GDN_CONV_CH), F32), pltpu.VMEM((t_len, GDN_CONV_CH), F32),
                        pltpu.VMEM((GDN_HEADS, GDN_KEY_DIM, GDN_VAL_DIM), F32)],
        compiler_params=pltpu.CompilerParams(dimension_semantics=("arbitrary", "arbitrary"),
                                             vmem_limit_bytes=VMEM_LIMIT),
        name="gated_deltanet_prompt",
    )(gqkv, s, gz, conv_w, conv0, s0, gnorm)


def _gdn_sample_kernel(xin_ref, s_ref, gz_ref, convw_ref, s0_ref, gnorm_ref, o_ref, sfin_ref, xbuf, ybuf,
                       *, n_seq, chunk):
    per_seq = SUBLANES + chunk
    n_in = n_seq * per_seq
    t_len = n_seq * chunk
    xbuf[0:SUBLANES, :] = jnp.zeros((SUBLANES, GDN_CONV_CH), F32)
    xbuf[SUBLANES:SUBLANES + n_in, :] = xin_ref[...]
    y = _short_conv(xbuf, convw_ref[...], n_in)
    for i in range(n_seq):
        ybuf[i * chunk:(i + 1) * chunk, :] = y[i * per_seq + SUBLANES:(i + 1) * per_seq]

    masks = _ChunkMasks(t_len, chunk)
    sblk = s_ref[...]
    cums = _chunk_cumsums(sblk, masks)
    seq_of_row = _div_pow2(_iota((t_len, GDN_KEY_DIM), 0), chunk)
    for h in range(GDN_HEADS):
        u, wmat, qk, q_dec, k_dec, chunk_decay = _delta_rule_chunk_terms(ybuf, sblk, cums, masks, h)
        v_news, reads = [], []
        for i in range(n_seq):
            rows = slice(i * chunk, (i + 1) * chunk)
            ws = _dot(jnp.concatenate([wmat[rows], q_dec[rows]], axis=0), s0_ref[i, h].astype(BF16))
            v_news.append(u[rows] - ws[:chunk])
            reads.append(ws[chunk:])
        v_new_bf = jnp.concatenate(v_news, axis=0).astype(BF16)
        o = jnp.concatenate(reads, axis=0) + _dot(qk, v_new_bf)
        for i in range(n_seq):
            own_rows = jnp.where(seq_of_row == i, k_dec, jnp.zeros_like(k_dec))
            sfin_ref[i, h] = (s0_ref[i, h] * chunk_decay[i * chunk:i * chunk + 1]
                              + _dot_tn(own_rows, v_new_bf))
        head_cols = slice(h * GDN_VAL_DIM, (h + 1) * GDN_VAL_DIM)
        o_ref[:, head_cols] = _gated_out_norm(o, gz_ref[:, head_cols], gnorm_ref[...])


def _gdn_sample(xin, s, gz, conv_w, s0, gnorm, n_seq_total, seq_per_step, chunk):
    steps = n_seq_total // seq_per_step
    per_seq = SUBLANES + chunk
    t_len = seq_per_step * chunk
    rows = lambda width: pl.BlockSpec((t_len, width), lambda i: (i, 0))
    const = lambda shape: pl.BlockSpec(shape, lambda i: (0,) * len(shape))
    state_spec = pl.BlockSpec((seq_per_step, GDN_HEADS, GDN_KEY_DIM, GDN_VAL_DIM), lambda i: (i, 0, 0, 0))
    return pl.pallas_call(
        functools.partial(_gdn_sample_kernel, n_seq=seq_per_step, chunk=chunk),
        grid=(steps,),
        in_specs=[pl.BlockSpec((seq_per_step * per_seq, GDN_CONV_CH), lambda i: (i, 0)), rows(LANES),
                  rows(GDN_V_WIDTH), const((GDN_CONV_WIDTH, GDN_CONV_CH)), state_spec, const((1, GDN_VAL_DIM))],
        out_specs=(rows(GDN_V_WIDTH), state_spec),
        out_shape=(jax.ShapeDtypeStruct((n_seq_total * chunk, GDN_V_WIDTH), F32),
                   jax.ShapeDtypeStruct((n_seq_total, GDN_HEADS, GDN_KEY_DIM, GDN_VAL_DIM), F32)),
        scratch_shapes=[pltpu.VMEM((seq_per_step * per_seq + SUBLANES, GDN_CONV_CH), F32),
                        pltpu.VMEM((t_len, GDN_CONV_CH), F32)],
        compiler_params=pltpu.CompilerParams(dimension_semantics=("arbitrary",), vmem_limit_bytes=VMEM_LIMIT),
        name="gated_deltanet_sample",
    )(xin, s, gz, conv_w, s0, gnorm)


def _post_kernel(x_ref, oa_ref, ob_ref, gates_ref, ple_ref, wa_ref, wb_ref, wout_ref, gmlp_ref, wup_ref,
                 wdown_ref, gple_ref, wpg_ref, wple_ref, y_ref):
    def rms(a, g):
        return a * lax.rsqrt(jnp.mean(a * a, axis=-1, keepdims=True) + NORM_EPS) * g

    gates = _sigmoid(gates_ref[...])
    merged = (gates[:, :D_MODEL] * _dot(oa_ref[...].astype(BF16), wa_ref[...])
              + gates[:, D_MODEL:] * _dot(ob_ref[...].astype(BF16), wb_ref[...]))
    x = x_ref[...] + _dot(merged.astype(BF16), wout_ref[...])
    h = rms(x, gmlp_ref[...]).astype(BF16)
    up = jnp.maximum(_dot(h, wup_ref[...]), 0.0)
    x = x + _dot((up * up).astype(BF16), wdown_ref[...])
    ple_gate = _sigmoid(_dot(rms(x, gple_ref[...]).astype(BF16), wpg_ref[...]))
    y_ref[...] = x + ple_gate * _dot(ple_ref[...].astype(BF16), wple_ref[...])


def _post(x, oa, ob, gates, ple, w, tm):
    n = x.shape[0]
    rows = lambda width: pl.BlockSpec((tm, width), lambda i: (i, 0))
    resident = lambda shape: pl.BlockSpec(shape, lambda i: (0, 0), pipeline_mode=pl.Buffered(1))
    return pl.pallas_call(
        _post_kernel,
        grid=(n // tm,),
        in_specs=[rows(D_MODEL), rows(FOX_WIDTH), rows(GDN_V_WIDTH), rows(2 * D_MODEL), rows(PLE_DIM),
                  resident((FOX_WIDTH, D_MODEL)), resident((GDN_V_WIDTH, D_MODEL)), resident((D_MODEL, D_MODEL)),
                  resident((1, D_MODEL)), resident((D_MODEL, D_FF)), resident((D_FF, D_MODEL)),
                  resident((1, D_MODEL)), resident((D_MODEL, D_MODEL)), resident((PLE_DIM, D_MODEL))],
        out_specs=rows(D_MODEL),
        out_shape=jax.ShapeDtypeStruct((n, D_MODEL), F32),
        compiler_params=pltpu.CompilerParams(dimension_semantics=("arbitrary",), vmem_limit_bytes=VMEM_LIMIT),
        name="merge_mlp_ple",
    )(x, oa, ob, gates, ple, w["w_a"], w["w_b"], w["w_out"], w["gmlp"], w["w_up"], w["w_down"], w["gple"],
      w["w_pg"], w["w_ple"])


def _prepare_weights(l, norm_mix_g, w_in, fox_f_bias, fox_q_norm_g, fox_k_norm_g, gdn_conv_w, gdn_a_log, gdn_dt_bias,
                     gdn_out_norm_g, w_branch_a, w_branch_b, w_out, norm_mlp_g, w_up, w_down, norm_ple_g,
                     w_ple_gate, w_ple):
    wi = w_in[l]
    o_ff = 3 * FOX_WIDTH
    o_gqkv = o_ff + FOX_HEADS
    o_ga = o_gqkv + GDN_CONV_CH
    o_gb = o_ga + GDN_HEADS
    o_gz = o_gb + GDN_HEADS
    o_gates = o_gz + GDN_V_WIDTH
    w_main = jnp.concatenate([wi[:, :o_ff], wi[:, o_gqkv:o_ga], wi[:, o_gz:]], axis=1).astype(BF16)
    w_small = jnp.concatenate([wi[:, o_ff:o_gqkv], wi[:, o_ga:o_gz]], axis=1)
    w_small = jnp.pad(w_small, ((0, 0), (0, LANES - N_SMALL))).astype(BF16)
    pad_col = lambda parts: jnp.pad(jnp.concatenate(parts), (0, LANES - N_SMALL)).reshape(LANES, 1).astype(F32)
    zeros_h = jnp.zeros((GDN_HEADS,), F32)
    head = jnp.arange(FOX_WIDTH) // FOX_HEAD_DIM
    return dict(
        gmix=norm_mix_g[l].reshape(1, D_MODEL),
        w_main=w_main,
        w_small=w_small,
        gmat=((head[:, None] == head[None, :]).astype(F32) / FOX_HEAD_DIM).astype(BF16),
        qg=jnp.tile(fox_q_norm_g[l], FOX_HEADS).reshape(1, FOX_WIDTH),
        kg=jnp.tile(fox_k_norm_g[l], FOX_HEADS).reshape(1, FOX_WIDTH),
        sbias=pad_col([fox_f_bias[l], gdn_dt_bias[l], zeros_h]),
        alog=pad_col([jnp.zeros((FOX_HEADS,), F32), gdn_a_log[l], zeros_h]),
        conv_w=gdn_conv_w[l],
        gnorm=gdn_out_norm_g[l].reshape(1, GDN_VAL_DIM),
        w_a=w_branch_a[l].astype(BF16),
        w_b=w_branch_b[l].astype(BF16),
        w_out=w_out[l].astype(BF16),
        gmlp=norm_mlp_g[l].reshape(1, D_MODEL),
        w_up=w_up[l].astype(BF16),
        w_down=w_down[l].astype(BF16),
        gple=norm_ple_g[l].reshape(1, D_MODEL),
        w_pg=w_ple_gate[l].astype(BF16),
        w_ple=w_ple[l].astype(BF16),
    )


def _pick_tile(n, target):
    t = min(n, target)
    while n % t:
        t //= 2
    return t


def _prompt_layer(x, ple, w):
    b, seq_len, _ = x.shape
    n = b * seq_len
    tm = _pick_tile(n, 256)
    q_bf, k_t, k_bf, v_t, v_bf, gqkv, gz, gates, s, st = _in_proj(x.reshape(n, D_MODEL), w, tm, kv_seq_len=seq_len)

    cum = _seq_cumsum(st, seq_len)
    ck = cum.reshape(FOX_HEADS, b, 1, seq_len).transpose(1, 0, 2, 3)
    cq = cum.reshape(FOX_HEADS, b, seq_len, 1).transpose(1, 0, 2, 3)
    tq = _pick_tile(seq_len, 512)
    o_a = _fox_prompt(q_bf.reshape(b, seq_len, FOX_WIDTH), k_bf.reshape(b, seq_len, FOX_WIDTH),
                      v_bf.reshape(b, seq_len, FOX_WIDTH), cq, ck, tq)

    chunk = math.gcd(seq_len, GDN_CHUNK)
    t_len = _pick_tile(seq_len, 2 * chunk)
    conv0 = jnp.zeros((b, SUBLANES, GDN_CONV_CH), F32)
    ssm0 = jnp.zeros((b, GDN_HEADS, GDN_KEY_DIM, GDN_VAL_DIM), F32)
    o_b, ssm = _gdn_prompt(gqkv, s, gz, w["conv_w"], conv0, ssm0, w["gnorm"], b, t_len, chunk)

    y = _post(x.reshape(n, D_MODEL), o_a.reshape(n, FOX_WIDTH), o_b, gates, ple.reshape(n, PLE_DIM), w, tm)
    keep = GDN_CONV_WIDTH - 1
    token_major = lambda a: a.reshape(b, FOX_HEADS, FOX_HEAD_DIM, seq_len).transpose(0, 3, 1, 2)
    states = (token_major(k_t), token_major(v_t),
              st[ROW_LOGF:ROW_LOGF + FOX_HEADS].reshape(FOX_HEADS, b, seq_len).transpose(1, 2, 0),
              gqkv.reshape(b, seq_len, GDN_CONV_CH)[:, seq_len - keep:], ssm)
    return y.reshape(b, seq_len, D_MODEL), states


def _sample_layer(x, ple, w, k_pool, v_pool, lf_pool, conv_buf, ssm_state, page_table):
    b, s_new, _ = x.shape
    n = b * s_new
    keep = GDN_CONV_WIDTH - 1
    assert s_new >= keep and s_new <= SUBLANES
    tm = _pick_tile(n, 256)
    q_bf, k, k_bf, v, v_bf, gqkv, gz, gates, s, st = _in_proj(x.reshape(n, D_MODEL), w, tm)

    n_pool, page = k_pool.shape[:2]
    q4 = q_bf.reshape(b, s_new, FOX_HEADS, 1, FOX_HEAD_DIM)
    eye = jnp.eye(FOX_HEADS, dtype=BF16).reshape(1, 1, FOX_HEADS, FOX_HEADS, 1)
    q_bd = (q4 * eye).reshape(b, s_new * FOX_HEADS, FOX_WIDTH)
    pad_keys = lambda a: jnp.pad(a.reshape(b, s_new, FOX_WIDTH).transpose(0, 2, 1),
                                 ((0, 0), (0, 0), (0, LANES - s_new)))
    lf_new_t = jnp.pad(st[ROW_LOGF:ROW_LOGF + FOX_HEADS].reshape(FOX_HEADS, b, s_new).transpose(1, 0, 2),
                       ((0, 0), (0, 0), (0, LANES - s_new)))
    pool_t = lambda a: a.transpose(0, 2, 3, 1).reshape(n_pool, FOX_WIDTH, page)
    o_a = _fox_sample(page_table, q_bd, pool_t(k_pool), pool_t(v_pool), lf_pool.transpose(0, 2, 1),
                      pad_keys(k_bf), pad_keys(v_bf), lf_new_t, pages_per_step=_pick_tile(page_table.shape[1], 8))

    chunk = SUBLANES
    pad_tok = lambda a: jnp.pad(a.reshape(b, s_new, -1), ((0, 0), (0, chunk - s_new), (0, 0)))
    xin = jnp.concatenate([jnp.pad(conv_buf, ((0, 0), (SUBLANES - keep, 0), (0, 0))), pad_tok(gqkv)], axis=1)
    seq_per_step = _pick_tile(b, LANES // chunk)
    o_b, ssm = _gdn_sample(xin.reshape(b * (SUBLANES + chunk), GDN_CONV_CH), pad_tok(s).reshape(b * chunk, LANES),
                           pad_tok(gz).reshape(b * chunk, GDN_V_WIDTH), w["conv_w"], ssm_state, w["gnorm"],
                           b, seq_per_step, chunk)
    o_b = o_b.reshape(b, chunk, GDN_V_WIDTH)[:, :s_new].reshape(n, GDN_V_WIDTH)

    y = _post(x.reshape(n, D_MODEL), o_a.reshape(n, FOX_WIDTH), o_b, gates, ple.reshape(n, PLE_DIM), w, tm)
    states = (k.reshape(b, s_new, FOX_HEADS, FOX_HEAD_DIM), v.reshape(b, s_new, FOX_HEADS, FOX_HEAD_DIM),
              s[:, ROW_LOGF:ROW_LOGF + FOX_HEADS].reshape(b, s_new, FOX_HEADS),
              gqkv.reshape(b, s_new, GDN_CONV_CH)[:, s_new - keep:], ssm)
    return y.reshape(b, s_new, D_MODEL), states


def kernel(x_prompt, x_sample, p_prompt, p_sample, cache_k, cache_v, cache_logf, state_conv, state_ssm, page_table,
           norm_mix_g, w_in, fox_f_bias, fox_q_norm_g, fox_k_norm_g, gdn_conv_w, gdn_a_log, gdn_dt_bias,
           gdn_out_norm_g, w_branch_a, w_branch_b, w_out, norm_mlp_g, w_up, w_down, norm_ple_g, w_ple_gate, w_ple):
    depth = w_in.shape[0]
    y_prompt, y_sample = x_prompt, x_sample
    prompt_states, sample_states = [], []
    for l in range(depth):
        w = _prepare_weights(l, norm_mix_g, w_in, fox_f_bias, fox_q_norm_g, fox_k_norm_g, gdn_conv_w, gdn_a_log,
                             gdn_dt_bias, gdn_out_norm_g, w_branch_a, w_branch_b, w_out, norm_mlp_g, w_up, w_down,
                             norm_ple_g, w_ple_gate, w_ple)
        y_prompt, st_p = _prompt_layer(y_prompt, p_prompt[l], w)
        y_sample, st_s = _sample_layer(y_sample, p_sample[l], w, cache_k[l], cache_v[l], cache_logf[l],
                                       state_conv[l], state_ssm[l], page_table)
        prompt_states.append(st_p)
        sample_states.append(st_s)
    stack = lambda states, i: jnp.stack([st[i] for st in states])
    return ((y_prompt, y_sample) + tuple(stack(prompt_states, i) for i in range(5))
            + tuple(stack(sample_states, i) for i in range(5)))
```

```python
import functools
import math

import jax
import jax.numpy as jnp
from jax import lax
from jax.experimental import pallas as pl
from jax.experimental.pallas import tpu as pltpu

F32 = jnp.float32
BF16 = jnp.bfloat16

D_MODEL = 1024
FOX_HEADS = 8
FOX_HEAD_DIM = 64
FOX_WIDTH = FOX_HEADS * FOX_HEAD_DIM
GDN_HEADS = 4
GDN_KEY_DIM = 128
GDN_VAL_DIM = 128
GDN_QK_WIDTH = GDN_HEADS * GDN_KEY_DIM
GDN_V_WIDTH = GDN_HEADS * GDN_VAL_DIM
GDN_CONV_WIDTH = 4
GDN_CONV_CH = 2 * GDN_QK_WIDTH + GDN_V_WIDTH
GDN_CHUNK = 64
D_FF = 4 * D_MODEL
PLE_DIM = 256
NORM_EPS = 1e-6

LANES = 128
SUBLANES = 8
NEG_BIG = -1e30
LOG2E = 1.4426950408889634
ZERO_PROB_LOG2 = 160.0
NORM_SLACK = 1.0 + 2.0 ** -7
VMEM_LIMIT = 56 * 1024 * 1024

_MAIN_SEGS = (FOX_WIDTH, FOX_WIDTH, FOX_WIDTH, GDN_CONV_CH, GDN_V_WIDTH, 2 * D_MODEL)
_MAIN_OFFS = tuple(sum(_MAIN_SEGS[:i]) for i in range(len(_MAIN_SEGS) + 1))
D_MAIN = _MAIN_OFFS[-1]
ROW_LOGF = 0
ROW_G = FOX_HEADS
ROW_BETA = FOX_HEADS + GDN_HEADS
N_SMALL = FOX_HEADS + 2 * GDN_HEADS


def _dot(a, b):
    return jnp.dot(a, b, preferred_element_type=F32)


def _dot_nt(a, b):
    return lax.dot_general(a, b, (((1,), (1,)), ((), ())), preferred_element_type=F32)


def _dot_tn(a, b):
    return lax.dot_general(a, b, (((0,), (0,)), ((), ())), preferred_element_type=F32)


def _split(a, parts, axis):
    pieces = []
    for _ in range(parts - 1):
        p = a.astype(BF16).astype(F32)
        pieces.append(p)
        a = a - p
    pieces.append(a)
    return jnp.concatenate(pieces, axis=axis).astype(BF16)


def _dot3(a, b):
    m, n = a.shape[0], b.shape[1]
    r = _dot(_split(a, 2, 0), _split(b, 2, 1))
    return (r[:m, :n] + r[:m, n:]) + (r[m:, :n] + r[m:, n:])


def _dot_exact_lhs(a_bf, b):
    n = b.shape[1]
    r = _dot(a_bf, _split(b, 3, 1))
    return r[:, :n] + (r[:, n:2 * n] + r[:, 2 * n:])


def _dot_exact_rhs(a, b_bf):
    m = a.shape[0]
    r = _dot(_split(a, 3, 0), b_bf)
    return r[:m] + (r[m:2 * m] + r[2 * m:])


def _sigmoid(x):
    return 1.0 / (1.0 + jnp.exp(-x))


def _iota(shape, dim):
    return lax.broadcasted_iota(jnp.int32, shape, dim)


def _div_pow2(x, divisor):
    shift = divisor.bit_length() - 1
    assert divisor == 1 << shift
    return lax.shift_right_logical(x, shift)


def _mod_pow2(x, divisor):
    assert divisor & (divisor - 1) == 0
    return x & (divisor - 1)


def _in_proj_kernel(x_ref, gmix_ref, wmain_ref, wsmall_ref, gmat_ref, qg_ref, kg_ref, sbias_ref, alog_ref,
                    q_ref, k_ref, kb_ref, v_ref, vb_ref, gqkv_ref, gz_ref, gates_ref, s_ref, st_ref,
                    *, feature_major_kv, q_scale):
    x = x_ref[...]
    xn = x * lax.rsqrt(jnp.mean(x * x, axis=-1, keepdims=True) + NORM_EPS) * gmix_ref[...]
    xb = xn.astype(BF16)

    def proj(seg):
        return _dot(xb, wmain_ref[:, _MAIN_OFFS[seg]:_MAIN_OFFS[seg + 1]])

    gmat = gmat_ref[...]

    def head_norm(t, g):
        r = _dot(_split(t * t, 2, 0), gmat)
        ms = r[:t.shape[0]] + r[t.shape[0]:]
        return t * lax.rsqrt(ms + NORM_EPS) * g

    q = head_norm(proj(0), qg_ref[...])
    q_ref[...] = (q * q_scale).astype(BF16)
    k = head_norm(proj(1), kg_ref[...])
    kb_ref[...] = k.astype(BF16)
    v = proj(2)
    vb_ref[...] = v.astype(BF16)
    if feature_major_kv:
        k_ref[0] = k.T
        v_ref[0] = v.T
    else:
        k_ref[...] = k
        v_ref[...] = v
    gqkv_ref[...] = proj(3)
    gz_ref[...] = proj(4)
    gates_ref[...] = proj(5)

    z = _dot(xb, wsmall_ref[...]).T + sbias_ref[...]
    row = _iota(z.shape, 0)
    t = jnp.log1p(jnp.exp(-jnp.abs(z)))
    logf = jnp.minimum(z, 0.0) - t
    g = -jnp.exp(alog_ref[...]) * (jnp.maximum(z, 0.0) + t)
    beta = _sigmoid(z)
    res = jnp.where(row < ROW_G, logf, jnp.where(row < ROW_BETA, g, jnp.where(row < N_SMALL, beta, 0.0)))
    st_ref[...] = res[:N_SMALL]
    s_ref[...] = res.T


def _in_proj(x, w, tm, q_scale, kv_seq_len=None):
    n = x.shape[0]
    const = lambda i: (0, 0)
    rows = lambda width: pl.BlockSpec((tm, width), lambda i: (i, 0))
    resident = lambda shape: pl.BlockSpec(shape, const, pipeline_mode=pl.Buffered(1))
    if kv_seq_len is None:
        kv_shape, kv_spec = jax.ShapeDtypeStruct((n, FOX_WIDTH), F32), rows(FOX_WIDTH)
    else:
        tiles = kv_seq_len // tm
        kv_shape = jax.ShapeDtypeStruct((n // kv_seq_len, FOX_WIDTH, kv_seq_len), F32)
        kv_spec = pl.BlockSpec((1, FOX_WIDTH, tm), lambda i: (i // tiles, 0, i % tiles))
    out_shape = (
        jax.ShapeDtypeStruct((n, FOX_WIDTH), BF16),
        kv_shape,
        jax.ShapeDtypeStruct((n, FOX_WIDTH), BF16),
        kv_shape,
        jax.ShapeDtypeStruct((n, FOX_WIDTH), BF16),
        jax.ShapeDtypeStruct((n, GDN_CONV_CH), F32),
        jax.ShapeDtypeStruct((n, GDN_V_WIDTH), F32),
        jax.ShapeDtypeStruct((n, 2 * D_MODEL), F32),
        jax.ShapeDtypeStruct((n, LANES), F32),
        jax.ShapeDtypeStruct((N_SMALL, n), F32),
    )
    out_specs = (
        rows(FOX_WIDTH), kv_spec, rows(FOX_WIDTH), kv_spec, rows(FOX_WIDTH),
        rows(GDN_CONV_CH), rows(GDN_V_WIDTH), rows(2 * D_MODEL), rows(LANES),
        pl.BlockSpec((N_SMALL, tm), lambda i: (0, i)),
    )
    in_specs = [
        rows(D_MODEL),
        resident((1, D_MODEL)),
        resident((D_MODEL, D_MAIN)),
        resident((D_MODEL, LANES)),
        resident((FOX_WIDTH, FOX_WIDTH)),
        resident((1, FOX_WIDTH)),
        resident((1, FOX_WIDTH)),
        resident((LANES, 1)),
        resident((LANES, 1)),
    ]
    return pl.pallas_call(
        functools.partial(_in_proj_kernel, feature_major_kv=kv_seq_len is not None, q_scale=q_scale),
        grid=(n // tm,),
        in_specs=in_specs,
        out_specs=out_specs,
        out_shape=out_shape,
        compiler_params=pltpu.CompilerParams(dimension_semantics=("arbitrary",), vmem_limit_bytes=VMEM_LIMIT),
        name="in_proj",
    )(x, w["gmix"], w["w_main"], w["w_small"], w["gmat"], w["qg"], w["kg"], w["sbias"], w["alog"])


def _cumsum_kernel(lf_ref, cum_ref, *, scale):
    n_chunks = lf_ref.shape[1] // LANES
    upper = (_iota((LANES, LANES), 0) <= _iota((LANES, LANES), 1)).astype(BF16)

    def body(c, carry):
        sl = pl.ds(pl.multiple_of(c * LANES, LANES), LANES)
        cs = carry + _dot_exact_rhs(lf_ref[:, sl], upper)
        cum_ref[:, sl] = cs * scale
        return jnp.broadcast_to(cs[:, LANES - 1:LANES], carry.shape)

    lax.fori_loop(0, n_chunks, body, jnp.zeros((FOX_HEADS, LANES), F32))


def _seq_cumsum(st, seq_len, scale):
    n = st.shape[1]
    spec = pl.BlockSpec((FOX_HEADS, seq_len), lambda b: (0, b))
    return pl.pallas_call(
        functools.partial(_cumsum_kernel, scale=scale),
        grid=(n // seq_len,),
        in_specs=[spec],
        out_specs=spec,
        out_shape=jax.ShapeDtypeStruct((FOX_HEADS, n), F32),
        compiler_params=pltpu.CompilerParams(dimension_semantics=("arbitrary",)),
        name="fox_cumsum",
    )(st)


def _fox_prompt_kernel(q_ref, k_ref, v_ref, cq_ref, ck_ref, o_ref, kmax_sc, *, tq):
    qi = pl.program_id(2)
    lane = _iota((tq, LANES), 1)
    first = lane < FOX_HEAD_DIM
    in_head = (first, jnp.logical_not(first))

    @pl.when(qi == 0)
    def _():
        def norms(c, carry):
            kk = k_ref[0, pl.ds(pl.multiple_of(c * tq, tq), tq), :].astype(F32)
            sq = kk * kk
            return tuple(jnp.maximum(carry[h], jnp.max(jnp.sum(jnp.where(in_head[h], sq, 0.0), axis=-1, keepdims=True),
                                                       axis=0, keepdims=True)) for h in range(2))
        zero = jnp.zeros((1, 1), F32)
        sq_max = lax.fori_loop(0, k_ref.shape[1] // tq, norms, (zero, zero))
        for h in range(2):
            kmax_sc[h] = jnp.broadcast_to(jnp.sqrt(sq_max[h]), kmax_sc.shape[1:])

    q2 = q_ref[0]
    causal = _iota((tq, tq), 0) >= _iota((tq, tq), 1)
    outs = []
    for h in range(2):
        qh = jnp.where(in_head[h], q2, jnp.zeros_like(q2))
        cq = cq_ref[0, h]
        qf = qh.astype(F32)
        q_norm = jnp.sqrt(jnp.sum(qf * qf, axis=-1, keepdims=True))
        reach = q_norm * kmax_sc[h][0:1, 0:1] * NORM_SLACK + cq

        def step(j, carry, diagonal):
            m, l, acc = carry
            sl = pl.ds(pl.multiple_of(j * tq, tq), tq)
            s = (cq - ck_ref[0, h, :, sl]) + _dot_nt(qh, k_ref[0, sl, :])
            if diagonal:
                s = jnp.where(causal, s, -jnp.inf)
            m_new = jnp.maximum(m, jnp.max(s, axis=-1, keepdims=True))
            alpha = jnp.exp2(m - m_new)
            p = jnp.exp2(s - m_new)
            l = alpha * l + jnp.sum(p, axis=-1, keepdims=True)
            acc = alpha * acc + _dot(p.astype(BF16), v_ref[0, sl, :])
            return m_new, l, acc

        def reaches(j, m):
            sl = pl.ds(pl.multiple_of(jnp.maximum(j, 0) * tq, tq), tq)
            return jnp.max(reach - m) - jnp.min(ck_ref[0, h, :, sl]) >= -ZERO_PROB_LOG2

        def cond(c):
            return jnp.logical_and(c[0] >= 0, c[1])

        def body(c):
            j, _, m, l, acc = c
            m, l, acc = step(j, (m, l, acc), diagonal=False)
            return j - 1, reaches(j - 1, m), m, l, acc

        col = lambda val: jnp.full((tq, 1), val, F32)
        m, l, acc = step(qi, (col(NEG_BIG), col(0.0), jnp.zeros((tq, LANES), F32)), diagonal=True)
        _, _, m, l, acc = lax.while_loop(cond, body, (qi - 1, reaches(qi - 1, m), m, l, acc))
        outs.append(acc * (1.0 / l))
    o_ref[0] = jnp.where(first, outs[0], outs[1])


def _fox_prompt(qb, kb, vb, cq, ck, tq):
    b, seq_len, _ = qb.shape
    pairs = FOX_HEADS // 2
    return pl.pallas_call(
        functools.partial(_fox_prompt_kernel, tq=tq),
        grid=(b, pairs, seq_len // tq),
        in_specs=[
            pl.BlockSpec((1, tq, LANES), lambda b_, p, i: (b_, i, p)),
            pl.BlockSpec((1, seq_len, LANES), lambda b_, p, i: (b_, 0, p)),
            pl.BlockSpec((1, seq_len, LANES), lambda b_, p, i: (b_, 0, p)),
            pl.BlockSpec((1, 2, tq, 1), lambda b_, p, i: (b_, p, i, 0)),
            pl.BlockSpec((1, 2, 1, seq_len), lambda b_, p, i: (b_, p, 0, 0)),
        ],
        out_specs=pl.BlockSpec((1, tq, LANES), lambda b_, p, i: (b_, i, p)),
        out_shape=jax.ShapeDtypeStruct((b, seq_len, FOX_WIDTH), F32),
        scratch_shapes=[pltpu.VMEM((2, SUBLANES, LANES), F32)],
        compiler_params=pltpu.CompilerParams(dimension_semantics=("arbitrary", "arbitrary", "arbitrary"),
                                             vmem_limit_bytes=VMEM_LIMIT),
        name="fox_prompt_attention",
    )(qb, kb, vb, cq, ck)


def _fox_sample_kernel(pt_ref, q_ref, *refs, pages_per_step, n_new):
    del pt_ref
    pp = pages_per_step
    k_refs, v_refs, lf_refs = refs[:pp], refs[pp:2 * pp], refs[2 * pp:3 * pp]
    knew_ref, vnew_ref, lfnew_ref, o_ref, m_sc, l_sc, acc_sc = refs[3 * pp:]
    step_id = pl.program_id(1)
    n_rows = n_new * FOX_HEADS

    @pl.when(step_id == 0)
    def _():
        m_sc[...] = jnp.full(m_sc.shape, NEG_BIG, F32)
        l_sc[...] = jnp.zeros(l_sc.shape, F32)
        acc_sc[...] = jnp.zeros(acc_sc.shape, F32)

    q = q_ref[0]
    ri = _iota((LANES, LANES), 0)
    ci = _iota((LANES, LANES), 1)
    after = (ri > ci).astype(BF16)
    upto = (ri <= ci).astype(BF16)
    tile_heads = lambda a: jnp.concatenate([a] * n_new, axis=0)

    def update(s, shift, v_t):
        m_prev = m_sc[...] + shift
        m_new = jnp.maximum(m_prev, jnp.max(s, axis=-1, keepdims=True))
        alpha = jnp.exp(m_prev - m_new)
        p = jnp.exp(s - m_new)
        l_sc[...] = alpha * l_sc[...] + jnp.sum(p, axis=-1, keepdims=True)
        acc_sc[...] = alpha * acc_sc[...] + _dot_nt(p.astype(BF16), v_t)
        m_sc[...] = m_new

    lf = jnp.concatenate([r[0] for r in lf_refs], axis=0)
    suffix = _dot_exact_rhs(lf, after)
    total = suffix[:, 0:1] + lf[:, 0:1]
    bias, later = [], jnp.zeros((FOX_HEADS, 1), F32)
    for j in reversed(range(pp)):
        rows = slice(j * FOX_HEADS, (j + 1) * FOX_HEADS)
        bias.append(tile_heads(suffix[rows] + later))
        later = later + total[rows]
    bias = jnp.concatenate(bias[::-1], axis=1)
    k_t = jnp.concatenate([r[0].astype(BF16) for r in k_refs], axis=1)
    v_t = jnp.concatenate([r[0].astype(BF16) for r in v_refs], axis=1)
    update(bias + _dot(q, k_t), tile_heads(later), v_t)

    @pl.when(step_id == pl.num_programs(1) - 1)
    def _():
        cum_rows = tile_heads(_dot_exact_rhs(lfnew_ref[0], upto))
        lane = _iota((n_rows, LANES), 1)
        query = _div_pow2(_iota((n_rows, LANES), 0), FOX_HEADS)
        cum_q = jnp.sum(jnp.where(lane == query, cum_rows, 0.0), axis=-1, keepdims=True)
        s = (cum_q - cum_rows) + _dot(q, knew_ref[0])
        s = jnp.where(lane <= query, s, -jnp.inf)
        update(s, cum_q, vnew_ref[0])
        out = acc_sc[...] / l_sc[...]
        own = _div_pow2(_iota(out.shape, 1), FOX_HEAD_DIM) == _mod_pow2(_iota(out.shape, 0), FOX_HEADS)
        out = jnp.where(own, out, 0.0)
        o_ref[0] = jnp.concatenate(
            [jnp.sum(out[i * FOX_HEADS:(i + 1) * FOX_HEADS], axis=0, keepdims=True) for i in range(n_new)], axis=0)


def _fox_sample(page_table, q_bd, k_pool_t, v_pool_t, lf_pool_t, k_new_t, v_new_t, lf_new_t, pages_per_step):
    n_seq, n_pages = page_table.shape
    n_rows = q_bd.shape[1]
    n_new = n_rows // FOX_HEADS
    page = k_pool_t.shape[2]
    pp = pages_per_step
    steps = n_pages // pp

    def paged(shape, j):
        return pl.BlockSpec(shape, lambda b, s, pt: (pt[b * n_pages + s * pp + j], 0, 0))

    per_seq = lambda shape: pl.BlockSpec(shape, lambda b, s, pt: (b, 0, 0))
    in_specs = ([per_seq((1, n_rows, FOX_WIDTH))]
                + [paged((1, FOX_WIDTH, page), j) for j in range(pp)]
                + [paged((1, FOX_WIDTH, page), j) for j in range(pp)]
                + [paged((1, FOX_HEADS, page), j) for j in range(pp)]
                + [per_seq((1, FOX_WIDTH, LANES)), per_seq((1, FOX_WIDTH, LANES)), per_seq((1, FOX_HEADS, LANES))])
    grid_spec = pltpu.PrefetchScalarGridSpec(
        num_scalar_prefetch=1,
        grid=(n_seq, steps),
        in_specs=in_specs,
        out_specs=per_seq((1, n_new, FOX_WIDTH)),
        scratch_shapes=[pltpu.VMEM((n_rows, 1), F32), pltpu.VMEM((n_rows, 1), F32),
                        pltpu.VMEM((n_rows, FOX_WIDTH), F32)],
    )
    return pl.pallas_call(
        functools.partial(_fox_sample_kernel, pages_per_step=pp, n_new=n_new),
        grid_spec=grid_spec,
        out_shape=jax.ShapeDtypeStruct((n_seq, n_new, FOX_WIDTH), F32),
        compiler_params=pltpu.CompilerParams(dimension_semantics=("arbitrary", "arbitrary"),
                                             vmem_limit_bytes=VMEM_LIMIT),
        name="fox_sample_attention",
    )(page_table.reshape(-1), q_bd, *([k_pool_t] * pp), *([v_pool_t] * pp), *([lf_pool_t] * pp),
      k_new_t, v_new_t, lf_new_t)


def _dot1(a, b):
    return _dot(a.astype(BF16), b.astype(BF16))


def _unit_lower_inverses(mats, same_block, chunk, base, dot):
    t_len = mats[0].shape[0]
    eye = (_iota((t_len, t_len), 0) == _iota((t_len, t_len), 1)).astype(F32)
    size = min(base, chunk)
    in_base = same_block(size)
    ns = [jnp.where(in_base, -a, 0.0) for a in mats]
    invs = [eye + n for n in ns]
    power = 1
    while 2 * power < size:
        ns = [dot(n, n) for n in ns]
        invs = [inv + dot(inv, n) for inv, n in zip(invs, ns)]
        power *= 2
    while size < chunk:
        off_mask = same_block(2 * size) & jnp.logical_not(same_block(size))
        corr = [dot(jnp.where(off_mask, a, 0.0), inv) for a, inv in zip(mats, invs)]
        invs = [inv - dot(inv, c) for inv, c in zip(invs, corr)]
        size *= 2
    return invs


def _short_conv(xbuf, conv_w, n_rows):
    first = SUBLANES - GDN_CONV_WIDTH + 1
    y = conv_w[0:1] * xbuf[first:first + n_rows, :]
    for i in range(1, GDN_CONV_WIDTH):
        y = y + conv_w[i:i + 1] * xbuf[first + i:first + i + n_rows, :]
    return y * _sigmoid(y)


class _ChunkMasks:
    def __init__(self, t_len, chunk):
        self.t_len, self.chunk = t_len, chunk
        ri = _iota((t_len, t_len), 0)
        ci = _iota((t_len, t_len), 1)
        self.same_block = lambda size: _div_pow2(ri, size) == _div_pow2(ci, size)
        in_chunk = self.same_block(chunk)
        self.lower = in_chunk & (ri >= ci)
        self.strict = in_chunk & (ri > ci)


def _chunk_cumsums(sblk, masks):
    chunk = masks.chunk
    cum_all = _dot_exact_lhs(masks.lower.astype(BF16), sblk)
    last_all = jnp.concatenate(
        [jnp.broadcast_to(cum_all[(c + 1) * chunk - 1:(c + 1) * chunk, :], (chunk, LANES))
         for c in range(masks.t_len // chunk)], axis=0)
    return cum_all, cum_all.T, last_all


def _delta_rule_chunk_terms(ybuf, sblks, cums, masks):
    sub = masks.t_len
    units = [(i, h) for i in range(len(sblks)) for h in range(GDN_HEADS)]
    col = lambda a, base, h: a[:, base + h:base + h + 1]
    betas = [col(sblks[i], ROW_BETA, h) for i, h in units]
    cumcols = [col(cums[i][0], ROW_G, h) for i, h in units]
    lasts = [col(cums[i][2], ROW_G, h) for i, h in units]
    decays = [jnp.where(masks.lower,
                        jnp.exp(jnp.where(masks.lower, cumcols[u] - cums[i][1][ROW_G + h:ROW_G + h + 1, :], 0.0)), 0.0)
              for u, (i, h) in enumerate(units)]
    qs, ks, vs = [], [], []
    for i, h in units:
        rows = slice(i * sub, (i + 1) * sub)
        q = ybuf[rows, h * GDN_KEY_DIM:(h + 1) * GDN_KEY_DIM]
        k = ybuf[rows, GDN_QK_WIDTH + h * GDN_KEY_DIM:GDN_QK_WIDTH + (h + 1) * GDN_KEY_DIM]
        qs.append(q * lax.rsqrt(jnp.sum(q * q, axis=-1, keepdims=True) + NORM_EPS) * (GDN_KEY_DIM ** -0.5))
        ks.append(k * lax.rsqrt(jnp.sum(k * k, axis=-1, keepdims=True) + NORM_EPS))
        vs.append(ybuf[rows, 2 * GDN_QK_WIDTH + h * GDN_VAL_DIM:2 * GDN_QK_WIDTH + (h + 1) * GDN_VAL_DIM])
    n = range(len(units))
    k_betas = [ks[u] * betas[u] for u in n]
    e_cums = [jnp.exp(cumcols[u]) for u in n]
    k_bfs = [ks[u].astype(BF16) for u in n]
    amats = [jnp.where(masks.strict, _dot_nt(k_betas[u].astype(BF16), k_bfs[u]) * decays[u], 0.0) for u in n]
    tmats = _unit_lower_inverses(amats, masks.same_block, masks.chunk, base=SUBLANES, dot=_dot1)
    uws = [_dot(tmats[u].astype(BF16),
                jnp.concatenate([vs[u] * betas[u], k_betas[u] * e_cums[u]], axis=-1).astype(BF16)) for u in n]
    qks = [(_dot_nt(qs[u].astype(BF16), k_bfs[u]) * decays[u]).astype(BF16) for u in n]
    flat = [(uws[u][:, :GDN_VAL_DIM], uws[u][:, GDN_VAL_DIM:].astype(BF16), qks[u], (qs[u] * e_cums[u]).astype(BF16),
             (ks[u] * jnp.exp(lasts[u] - cumcols[u])).astype(BF16), jnp.exp(lasts[u])) for u in n]
    return [flat[i * GDN_HEADS:(i + 1) * GDN_HEADS] for i in range(len(sblks))]


def _gated_out_norm(o, gz, gnorm):
    o = o * lax.rsqrt(jnp.mean(o * o, axis=-1, keepdims=True) + NORM_EPS) * gnorm
    return o * (gz * _sigmoid(gz))


def _gdn_prompt_kernel(gqkv_ref, s_ref, gz_ref, convw_ref, conv0_ref, s0_ref, gnorm_ref, o_ref, sfin_ref,
                       xbuf, ybuf, state, *, t_len, chunk):
    t = pl.program_id(1)

    @pl.when(t == 0)
    def _():
        xbuf[0:SUBLANES, :] = conv0_ref[0]
        state[...] = s0_ref[0]

    xbuf[SUBLANES:SUBLANES + t_len, :] = gqkv_ref[...]
    ybuf[...] = _short_conv(xbuf, convw_ref[...], t_len)
    xbuf[0:SUBLANES, :] = xbuf[t_len:t_len + SUBLANES, :]

    sub = min(t_len, LANES)
    masks = _ChunkMasks(sub, chunk)
    sblks = [s_ref[i * sub:(i + 1) * sub, :] for i in range(t_len // sub)]
    terms = _delta_rule_chunk_terms(ybuf, sblks, [_chunk_cumsums(sb, masks) for sb in sblks], masks)
    outs = [[] for _ in range(GDN_HEADS)]
    for sub_terms in terms:
        for c in range(sub // chunk):
            rows = slice(c * chunk, (c + 1) * chunk)
            for h, (u, wmat, qk, q_dec, k_dec, chunk_decay) in enumerate(sub_terms):
                st = state[h]
                ws = _dot(jnp.concatenate([wmat[rows], q_dec[rows]], axis=0), st.astype(BF16))
                v_new_bf = (u[rows] - ws[:chunk]).astype(BF16)
                outs[h].append(ws[chunk:] + _dot(qk[rows, rows], v_new_bf))
                state[h] = st * chunk_decay[c * chunk:c * chunk + 1] + _dot_tn(k_dec[rows], v_new_bf)
    for h in range(GDN_HEADS):
        head_cols = slice(h * GDN_VAL_DIM, (h + 1) * GDN_VAL_DIM)
        o_ref[:, head_cols] = _gated_out_norm(jnp.concatenate(outs[h], axis=0), gz_ref[:, head_cols], gnorm_ref[...])

    @pl.when(t == pl.num_programs(1) - 1)
    def _():
        sfin_ref[0] = state[...]


def _gdn_prompt(gqkv, s, gz, conv_w, conv0, s0, gnorm, n_seq, t_len, chunk):
    n = gqkv.shape[0]
    tiles = n // n_seq // t_len
    rows = lambda width: pl.BlockSpec((t_len, width), lambda b, t: (b * tiles + t, 0))
    const = lambda shape: pl.BlockSpec(shape, lambda b, t: (0,) * len(shape))
    state_spec = pl.BlockSpec((1, GDN_HEADS, GDN_KEY_DIM, GDN_VAL_DIM), lambda b, t: (b, 0, 0, 0))
    return pl.pallas_call(
        functools.partial(_gdn_prompt_kernel, t_len=t_len, chunk=chunk),
        grid=(n_seq, tiles),
        in_specs=[rows(GDN_CONV_CH), rows(LANES), rows(GDN_V_WIDTH), const((GDN_CONV_WIDTH, GDN_CONV_CH)),
                  pl.BlockSpec((1, SUBLANES, GDN_CONV_CH), lambda b, t: (b, 0, 0)), state_spec,
                  const((1, GDN_VAL_DIM))],
        out_specs=(rows(GDN_V_WIDTH), state_spec),
        out_shape=(jax.ShapeDtypeStruct((n, GDN_V_WIDTH), F32),
                   jax.ShapeDtypeStruct((n_seq, GDN_HEADS, GDN_KEY_DIM, GDN_VAL_DIM), F32)),
        scratch_shapes=[pltpu.VMEM((t_len + SUBLANES, GDN_CONV_CH), F32), pltpu.VMEM((t_len, GDN_CONV_CH), F32),
                        pltpu.VMEM((GDN_HEADS, GDN_KEY_DIM, GDN_VAL_DIM), F32)],
        compiler_params=pltpu.CompilerParams(dimension_semantics=("arbitrary", "arbitrary"),
                                             vmem_limit_bytes=VMEM_LIMIT),
        name="gated_deltanet_prompt",
    )(gqkv, s, gz, conv_w, conv0, s0, gnorm)


def _gdn_sample_kernel(xin_ref, s_ref, gz_ref, convw_ref, s0_ref, gnorm_ref, o_ref, sfin_ref, xbuf, ybuf,
                       *, n_seq, chunk):
    per_seq = SUBLANES + chunk
    n_in = n_seq * per_seq
    t_len = n_seq * chunk
    xbuf[0:SUBLANES, :] = jnp.zeros((SUBLANES, GDN_CONV_CH), F32)
    xbuf[SUBLANES:SUBLANES + n_in, :] = xin_ref[...]
    y = _short_conv(xbuf, convw_ref[...], n_in)
    for i in range(n_seq):
        ybuf[i * chunk:(i + 1) * chunk, :] = y[i * per_seq + SUBLANES:(i + 1) * per_seq]

    masks = _ChunkMasks(t_len, chunk)
    sblk = s_ref[...]
    cums = _chunk_cumsums(sblk, masks)
    seq_of_row = _div_pow2(_iota((t_len, GDN_KEY_DIM), 0), chunk)
    terms = _delta_rule_chunk_terms(ybuf, [sblk], [cums], masks)[0]
    for h, (u, wmat, qk, q_dec, k_dec, chunk_decay) in enumerate(terms):
        v_news, reads = [], []
        for i in range(n_seq):
            rows = slice(i * chunk, (i + 1) * chunk)
            ws = _dot(jnp.concatenate([wmat[rows], q_dec[rows]], axis=0), s0_ref[i, h].astype(BF16))
            v_news.append(u[rows] - ws[:chunk])
            reads.append(ws[chunk:])
        v_new_bf = jnp.concatenate(v_news, axis=0).astype(BF16)
        o = jnp.concatenate(reads, axis=0) + _dot(qk, v_new_bf)
        for i in range(n_seq):
            own_rows = jnp.where(seq_of_row == i, k_dec, jnp.zeros_like(k_dec))
            sfin_ref[i, h] = (s0_ref[i, h] * chunk_decay[i * chunk:i * chunk + 1]
                              + _dot_tn(own_rows, v_new_bf))
        head_cols = slice(h * GDN_VAL_DIM, (h + 1) * GDN_VAL_DIM)
        o_ref[:, head_cols] = _gated_out_norm(o, gz_ref[:, head_cols], gnorm_ref[...])


def _gdn_sample(xin, s, gz, conv_w, s0, gnorm, n_seq_total, seq_per_step, chunk):
    steps = n_seq_total // seq_per_step
    per_seq = SUBLANES + chunk
    t_len = seq_per_step * chunk
    rows = lambda width: pl.BlockSpec((t_len, width), lambda i: (i, 0))
    const = lambda shape: pl.BlockSpec(shape, lambda i: (0,) * len(shape))
    state_spec = pl.BlockSpec((seq_per_step, GDN_HEADS, GDN_KEY_DIM, GDN_VAL_DIM), lambda i: (i, 0, 0, 0))
    return pl.pallas_call(
        functools.partial(_gdn_sample_kernel, n_seq=seq_per_step, chunk=chunk),
        grid=(steps,),
        in_specs=[pl.BlockSpec((seq_per_step * per_seq, GDN_CONV_CH), lambda i: (i, 0)), rows(LANES),
                  rows(GDN_V_WIDTH), const((GDN_CONV_WIDTH, GDN_CONV_CH)), state_spec, const((1, GDN_VAL_DIM))],
        out_specs=(rows(GDN_V_WIDTH), state_spec),
        out_shape=(jax.ShapeDtypeStruct((n_seq_total * chunk, GDN_V_WIDTH), F32),
                   jax.ShapeDtypeStruct((n_seq_total, GDN_HEADS, GDN_KEY_DIM, GDN_VAL_DIM), F32)),
        scratch_shapes=[pltpu.VMEM((seq_per_step * per_seq + SUBLANES, GDN_CONV_CH), F32),
                        pltpu.VMEM((t_len, GDN_CONV_CH), F32)],
        compiler_params=pltpu.CompilerParams(dimension_semantics=("arbitrary",), vmem_limit_bytes=VMEM_LIMIT),
        name="gated_deltanet_sample",
    )(xin, s, gz, conv_w, s0, gnorm)


def _post_kernel(x_ref, oa_ref, ob_ref, gates_ref, ple_ref, wa_ref, wb_ref, wout_ref, gmlp_ref, wup_ref,
                 wdown_ref, gple_ref, wpg_ref, wple_ref, y_ref):
    def rms(a, g):
        return a * lax.rsqrt(jnp.mean(a * a, axis=-1, keepdims=True) + NORM_EPS) * g

    gates = _sigmoid(gates_ref[...])
    merged = (gates[:, :D_MODEL] * _dot(oa_ref[...].astype(BF16), wa_ref[...])
              + gates[:, D_MODEL:] * _dot(ob_ref[...].astype(BF16), wb_ref[...]))
    x = x_ref[...] + _dot(merged.astype(BF16), wout_ref[...])
    h = rms(x, gmlp_ref[...]).astype(BF16)
    up = jnp.maximum(_dot(h, wup_ref[...]), 0.0)
    x = x + _dot((up * up).astype(BF16), wdown_ref[...])
    ple_gate = _sigmoid(_dot(rms(x, gple_ref[...]).astype(BF16), wpg_ref[...]))
    y_ref[...] = x + ple_gate * _dot(ple_ref[...].astype(BF16), wple_ref[...])


def _post(x, oa, ob, gates, ple, w, tm):
    n = x.shape[0]
    rows = lambda width: pl.BlockSpec((tm, width), lambda i: (i, 0))
    resident = lambda shape: pl.BlockSpec(shape, lambda i: (0, 0), pipeline_mode=pl.Buffered(1))
    return pl.pallas_call(
        _post_kernel,
        grid=(n // tm,),
        in_specs=[rows(D_MODEL), rows(FOX_WIDTH), rows(GDN_V_WIDTH), rows(2 * D_MODEL), rows(PLE_DIM),
                  resident((FOX_WIDTH, D_MODEL)), resident((GDN_V_WIDTH, D_MODEL)), resident((D_MODEL, D_MODEL)),
                  resident((1, D_MODEL)), resident((D_MODEL, D_FF)), resident((D_FF, D_MODEL)),
                  resident((1, D_MODEL)), resident((D_MODEL, D_MODEL)), resident((PLE_DIM, D_MODEL))],
        out_specs=rows(D_MODEL),
        out_shape=jax.ShapeDtypeStruct((n, D_MODEL), F32),
        compiler_params=pltpu.CompilerParams(dimension_semantics=("arbitrary",), vmem_limit_bytes=VMEM_LIMIT),
        name="merge_mlp_ple",
    )(x, oa, ob, gates, ple, w["w_a"], w["w_b"], w["w_out"], w["gmlp"], w["w_up"], w["w_down"], w["gple"],
      w["w_pg"], w["w_ple"])


def _prepare_weights(l, norm_mix_g, w_in, fox_f_bias, fox_q_norm_g, fox_k_norm_g, gdn_conv_w, gdn_a_log, gdn_dt_bias,
                     gdn_out_norm_g, w_branch_a, w_branch_b, w_out, norm_mlp_g, w_up, w_down, norm_ple_g,
                     w_ple_gate, w_ple):
    wi = w_in[l]
    o_ff = 3 * FOX_WIDTH
    o_gqkv = o_ff + FOX_HEADS
    o_ga = o_gqkv + GDN_CONV_CH
    o_gb = o_ga + GDN_HEADS
    o_gz = o_gb + GDN_HEADS
    o_gates = o_gz + GDN_V_WIDTH
    w_main = jnp.concatenate([wi[:, :o_ff], wi[:, o_gqkv:o_ga], wi[:, o_gz:]], axis=1).astype(BF16)
    w_small = jnp.concatenate([wi[:, o_ff:o_gqkv], wi[:, o_ga:o_gz]], axis=1)
    w_small = jnp.pad(w_small, ((0, 0), (0, LANES - N_SMALL))).astype(BF16)
    pad_col = lambda parts: jnp.pad(jnp.concatenate(parts), (0, LANES - N_SMALL)).reshape(LANES, 1).astype(F32)
    zeros_h = jnp.zeros((GDN_HEADS,), F32)
    head = jnp.arange(FOX_WIDTH) // FOX_HEAD_DIM
    return dict(
        gmix=norm_mix_g[l].reshape(1, D_MODEL),
        w_main=w_main,
        w_small=w_small,
        gmat=((head[:, None] == head[None, :]).astype(F32) / FOX_HEAD_DIM).astype(BF16),
        qg=jnp.tile(fox_q_norm_g[l], FOX_HEADS).reshape(1, FOX_WIDTH),
        kg=jnp.tile(fox_k_norm_g[l], FOX_HEADS).reshape(1, FOX_WIDTH),
        sbias=pad_col([fox_f_bias[l], gdn_dt_bias[l], zeros_h]),
        alog=pad_col([jnp.zeros((FOX_HEADS,), F32), gdn_a_log[l], zeros_h]),
        conv_w=gdn_conv_w[l],
        gnorm=gdn_out_norm_g[l].reshape(1, GDN_VAL_DIM),
        w_a=w_branch_a[l].astype(BF16),
        w_b=w_branch_b[l].astype(BF16),
        w_out=w_out[l].astype(BF16),
        gmlp=norm_mlp_g[l].reshape(1, D_MODEL),
        w_up=w_up[l].astype(BF16),
        w_down=w_down[l].astype(BF16),
        gple=norm_ple_g[l].reshape(1, D_MODEL),
        w_pg=w_ple_gate[l].astype(BF16),
        w_ple=w_ple[l].astype(BF16),
    )


def _pick_tile(n, target):
    t = min(n, target)
    while n % t:
        t //= 2
    return t


def _prompt_layer(x, ple, w):
    b, seq_len, _ = x.shape
    n = b * seq_len
    tm = _pick_tile(n, 256)
    q_bf, k_t, k_bf, v_t, v_bf, gqkv, gz, gates, s, st = _in_proj(
        x.reshape(n, D_MODEL), w, tm, FOX_HEAD_DIM ** -0.5 * LOG2E, kv_seq_len=seq_len)

    cum = _seq_cumsum(st, seq_len, LOG2E)
    ck = cum.reshape(FOX_HEADS, b, 1, seq_len).transpose(1, 0, 2, 3)
    cq = cum.reshape(FOX_HEADS, b, seq_len, 1).transpose(1, 0, 2, 3)
    tq = _pick_tile(seq_len, 512)
    o_a = _fox_prompt(q_bf.reshape(b, seq_len, FOX_WIDTH), k_bf.reshape(b, seq_len, FOX_WIDTH),
                      v_bf.reshape(b, seq_len, FOX_WIDTH), cq, ck, tq)

    chunk = math.gcd(seq_len, GDN_CHUNK)
    t_len = _pick_tile(seq_len, 8 * chunk)
    conv0 = jnp.zeros((b, SUBLANES, GDN_CONV_CH), F32)
    ssm0 = jnp.zeros((b, GDN_HEADS, GDN_KEY_DIM, GDN_VAL_DIM), F32)
    o_b, ssm = _gdn_prompt(gqkv, s, gz, w["conv_w"], conv0, ssm0, w["gnorm"], b, t_len, chunk)

    y = _post(x.reshape(n, D_MODEL), o_a.reshape(n, FOX_WIDTH), o_b, gates, ple.reshape(n, PLE_DIM), w, tm)
    keep = GDN_CONV_WIDTH - 1
    token_major = lambda a: a.reshape(b, FOX_HEADS, FOX_HEAD_DIM, seq_len).transpose(0, 3, 1, 2)
    states = (token_major(k_t), token_major(v_t),
              st[ROW_LOGF:ROW_LOGF + FOX_HEADS].reshape(FOX_HEADS, b, seq_len).transpose(1, 2, 0),
              gqkv.reshape(b, seq_len, GDN_CONV_CH)[:, seq_len - keep:], ssm)
    return y.reshape(b, seq_len, D_MODEL), states


def _sample_layer(x, ple, w, k_pool, v_pool, lf_pool, conv_buf, ssm_state, page_table):
    b, s_new, _ = x.shape
    n = b * s_new
    keep = GDN_CONV_WIDTH - 1
    assert s_new >= keep and s_new <= SUBLANES
    tm = _pick_tile(n, 256)
    q_bf, k, k_bf, v, v_bf, gqkv, gz, gates, s, st = _in_proj(x.reshape(n, D_MODEL), w, tm, FOX_HEAD_DIM ** -0.5)

    n_pool, page = k_pool.shape[:2]
    q4 = q_bf.reshape(b, s_new, FOX_HEADS, 1, FOX_HEAD_DIM)
    eye = jnp.eye(FOX_HEADS, dtype=BF16).reshape(1, 1, FOX_HEADS, FOX_HEADS, 1)
    q_bd = (q4 * eye).reshape(b, s_new * FOX_HEADS, FOX_WIDTH)
    pad_keys = lambda a: jnp.pad(a.reshape(b, s_new, FOX_WIDTH).transpose(0, 2, 1),
                                 ((0, 0), (0, 0), (0, LANES - s_new)))
    lf_new_t = jnp.pad(st[ROW_LOGF:ROW_LOGF + FOX_HEADS].reshape(FOX_HEADS, b, s_new).transpose(1, 0, 2),
                       ((0, 0), (0, 0), (0, LANES - s_new)))
    pool_t = lambda a: a.transpose(0, 2, 3, 1).reshape(n_pool, FOX_WIDTH, page)
    o_a = _fox_sample(page_table, q_bd, pool_t(k_pool), pool_t(v_pool), lf_pool.transpose(0, 2, 1),
                      pad_keys(k_bf), pad_keys(v_bf), lf_new_t, pages_per_step=_pick_tile(page_table.shape[1], 8))

    chunk = SUBLANES
    pad_tok = lambda a: jnp.pad(a.reshape(b, s_new, -1), ((0, 0), (0, chunk - s_new), (0, 0)))
    xin = jnp.concatenate([jnp.pad(conv_buf, ((0, 0), (SUBLANES - keep, 0), (0, 0))), pad_tok(gqkv)], axis=1)
    seq_per_step = _pick_tile(b, LANES // chunk)
    o_b, ssm = _gdn_sample(xin.reshape(b * (SUBLANES + chunk), GDN_CONV_CH), pad_tok(s).reshape(b * chunk, LANES),
                           pad_tok(gz).reshape(b * chunk, GDN_V_WIDTH), w["conv_w"], ssm_state, w["gnorm"],
                           b, seq_per_step, chunk)
    o_b = o_b.reshape(b, chunk, GDN_V_WIDTH)[:, :s_new].reshape(n, GDN_V_WIDTH)

    y = _post(x.reshape(n, D_MODEL), o_a.reshape(n, FOX_WIDTH), o_b, gates, ple.reshape(n, PLE_DIM), w, tm)
    states = (k.reshape(b, s_new, FOX_HEADS, FOX_HEAD_DIM), v.reshape(b, s_new, FOX_HEADS, FOX_HEAD_DIM),
              s[:, ROW_LOGF:ROW_LOGF + FOX_HEADS].reshape(b, s_new, FOX_HEADS),
              gqkv.reshape(b, s_new, GDN_CONV_CH)[:, s_new - keep:], ssm)
    return y.reshape(b, s_new, D_MODEL), states


def kernel(x_prompt, x_sample, p_prompt, p_sample, cache_k, cache_v, cache_logf, state_conv, state_ssm, page_table,
           norm_mix_g, w_in, fox_f_bias, fox_q_norm_g, fox_k_norm_g, gdn_conv_w, gdn_a_log, gdn_dt_bias,
           gdn_out_norm_g, w_branch_a, w_branch_b, w_out, norm_mlp_g, w_up, w_down, norm_ple_g, w_ple_gate, w_ple):
    depth = w_in.shape[0]
    y_prompt, y_sample = x_prompt, x_sample
    prompt_states, sample_states = [], []
    for l in range(depth):
        w = _prepare_weights(l, norm_mix_g, w_in, fox_f_bias, fox_q_norm_g, fox_k_norm_g, gdn_conv_w, gdn_a_log,
                             gdn_dt_bias, gdn_out_norm_g, w_branch_a, w_branch_b, w_out, norm_mlp_g, w_up, w_down,
                             norm_ple_g, w_ple_gate, w_ple)
        y_prompt, st_p = _prompt_layer(y_prompt, p_prompt[l], w)
        y_sample, st_s = _sample_layer(y_sample, p_sample[l], w, cache_k[l], cache_v[l], cache_logf[l],
                                       state_conv[l], state_ssm[l], page_table)
        prompt_states.append(st_p)
        sample_states.append(st_s)
    stack = lambda states, i: jnp.stack([st[i] for st in states])
    return ((y_prompt, y_sample) + tuple(stack(prompt_states, i) for i in range(5))
            + tuple(stack(sample_states, i) for i in range(5)))
```

```python
import functools
import math

import jax
import jax.numpy as jnp
from jax import lax
from jax.experimental import pallas as pl
from jax.experimental.pallas import tpu as pltpu

F32 = jnp.float32
BF16 = jnp.bfloat16

D_MODEL = 1024
FOX_HEADS = 8
FOX_HEAD_DIM = 64
FOX_WIDTH = FOX_HEADS * FOX_HEAD_DIM
GDN_HEADS = 4
GDN_KEY_DIM = 128
GDN_VAL_DIM = 128
GDN_QK_WIDTH = GDN_HEADS * GDN_KEY_DIM
GDN_V_WIDTH = GDN_HEADS * GDN_VAL_DIM
GDN_CONV_WIDTH = 4
GDN_CONV_CH = 2 * GDN_QK_WIDTH + GDN_V_WIDTH
GDN_CHUNK = 64
D_FF = 4 * D_MODEL
PLE_DIM = 256
NORM_EPS = 1e-6

LANES = 128
SUBLANES = 8
NEG_BIG = -1e30
LOG2E = 1.4426950408889634
ZERO_PROB_LOG2 = 160.0
NORM_SLACK = 1.0 + 2.0 ** -7
VMEM_LIMIT = 56 * 1024 * 1024

_MAIN_SEGS = (FOX_WIDTH, FOX_WIDTH, FOX_WIDTH, GDN_CONV_CH, GDN_V_WIDTH, 2 * D_MODEL)
_MAIN_OFFS = tuple(sum(_MAIN_SEGS[:i]) for i in range(len(_MAIN_SEGS) + 1))
D_MAIN = _MAIN_OFFS[-1]
ROW_LOGF = 0
ROW_G = FOX_HEADS
ROW_BETA = FOX_HEADS + GDN_HEADS
N_SMALL = FOX_HEADS + 2 * GDN_HEADS


def _dot(a, b):
    return jnp.dot(a, b, preferred_element_type=F32)


def _dot_nt(a, b):
    return lax.dot_general(a, b, (((1,), (1,)), ((), ())), preferred_element_type=F32)


def _dot_tn(a, b):
    return lax.dot_general(a, b, (((0,), (0,)), ((), ())), preferred_element_type=F32)


def _split(a, parts, axis):
    pieces = []
    for _ in range(parts - 1):
        p = a.astype(BF16).astype(F32)
        pieces.append(p)
        a = a - p
    pieces.append(a)
    return jnp.concatenate(pieces, axis=axis).astype(BF16)


def _dot3(a, b):
    m, n = a.shape[0], b.shape[1]
    r = _dot(_split(a, 2, 0), _split(b, 2, 1))
    return (r[:m, :n] + r[:m, n:]) + (r[m:, :n] + r[m:, n:])


def _dot_exact_lhs(a_bf, b):
    n = b.shape[1]
    r = _dot(a_bf, _split(b, 3, 1))
    return r[:, :n] + (r[:, n:2 * n] + r[:, 2 * n:])


def _dot_exact_rhs(a, b_bf):
    m = a.shape[0]
    r = _dot(_split(a, 3, 0), b_bf)
    return r[:m] + (r[m:2 * m] + r[2 * m:])


def _sigmoid(x):
    return 1.0 / (1.0 + jnp.exp(-x))


def _iota(shape, dim):
    return lax.broadcasted_iota(jnp.int32, shape, dim)


def _div_pow2(x, divisor):
    shift = divisor.bit_length() - 1
    assert divisor == 1 << shift
    return lax.shift_right_logical(x, shift)


def _mod_pow2(x, divisor):
    assert divisor & (divisor - 1) == 0
    return x & (divisor - 1)


def _in_proj_kernel(x_ref, gmix_ref, wmain_ref, wsmall_ref, gmat_ref, qg_ref, kg_ref, sbias_ref, alog_ref,
                    q_ref, k_ref, kb_ref, v_ref, vb_ref, gqkv_ref, gz_ref, gates_ref, s_ref, st_ref,
                    *, feature_major_kv, q_scale):
    x = x_ref[...]
    xn = x * lax.rsqrt(jnp.mean(x * x, axis=-1, keepdims=True) + NORM_EPS) * gmix_ref[...]
    xb = xn.astype(BF16)

    def proj(seg):
        return _dot(xb, wmain_ref[:, _MAIN_OFFS[seg]:_MAIN_OFFS[seg + 1]])

    gmat = gmat_ref[...]

    def head_norm(t, g):
        ms = _dot((t * t).astype(BF16), gmat)
        return t * lax.rsqrt(ms + NORM_EPS) * g

    q = head_norm(proj(0), qg_ref[...])
    q_ref[...] = (q * q_scale).astype(BF16)
    k = head_norm(proj(1), kg_ref[...])
    kb_ref[...] = k.astype(BF16)
    v = proj(2)
    vb_ref[...] = v.astype(BF16)
    if feature_major_kv:
        k_ref[0] = k.T
        v_ref[0] = v.T
    else:
        k_ref[...] = k
        v_ref[...] = v
    gqkv_ref[...] = proj(3)
    gz_ref[...] = proj(4)
    gates_ref[...] = proj(5).astype(gates_ref.dtype)

    z = _dot(xb, wsmall_ref[...]).T + sbias_ref[...]
    row = _iota(z.shape, 0)
    t = jnp.log1p(jnp.exp(-jnp.abs(z)))
    logf = jnp.minimum(z, 0.0) - t
    g = -jnp.exp(alog_ref[...]) * (jnp.maximum(z, 0.0) + t)
    beta = _sigmoid(z)
    res = jnp.where(row < ROW_G, logf, jnp.where(row < ROW_BETA, g, jnp.where(row < N_SMALL, beta, 0.0)))
    st_ref[...] = res[:N_SMALL]
    s_ref[...] = res.T


def _in_proj(x, w, tm, q_scale, kv_seq_len=None):
    n = x.shape[0]
    const = lambda i: (0, 0)
    rows = lambda width: pl.BlockSpec((tm, width), lambda i: (i, 0))
    resident = lambda shape: pl.BlockSpec(shape, const, pipeline_mode=pl.Buffered(1))
    if kv_seq_len is None:
        kv_shape, kv_spec = jax.ShapeDtypeStruct((n, FOX_WIDTH), F32), rows(FOX_WIDTH)
    else:
        tiles = kv_seq_len // tm
        kv_shape = jax.ShapeDtypeStruct((n // kv_seq_len, FOX_WIDTH, kv_seq_len), F32)
        kv_spec = pl.BlockSpec((1, FOX_WIDTH, tm), lambda i: (i // tiles, 0, i % tiles))
    out_shape = (
        jax.ShapeDtypeStruct((n, FOX_WIDTH), BF16),
        kv_shape,
        jax.ShapeDtypeStruct((n, FOX_WIDTH), BF16),
        kv_shape,
        jax.ShapeDtypeStruct((n, FOX_WIDTH), BF16),
        jax.ShapeDtypeStruct((n, GDN_CONV_CH), F32),
        jax.ShapeDtypeStruct((n, GDN_V_WIDTH), F32),
        jax.ShapeDtypeStruct((n, 2 * D_MODEL), BF16),
        jax.ShapeDtypeStruct((n, LANES), F32),
        jax.ShapeDtypeStruct((N_SMALL, n), F32),
    )
    out_specs = (
        rows(FOX_WIDTH), kv_spec, rows(FOX_WIDTH), kv_spec, rows(FOX_WIDTH),
        rows(GDN_CONV_CH), rows(GDN_V_WIDTH), rows(2 * D_MODEL), rows(LANES),
        pl.BlockSpec((N_SMALL, tm), lambda i: (0, i)),
    )
    in_specs = [
        rows(D_MODEL),
        resident((1, D_MODEL)),
        resident((D_MODEL, D_MAIN)),
        resident((D_MODEL, LANES)),
        resident((FOX_WIDTH, FOX_WIDTH)),
        resident((1, FOX_WIDTH)),
        resident((1, FOX_WIDTH)),
        resident((LANES, 1)),
        resident((LANES, 1)),
    ]
    return pl.pallas_call(
        functools.partial(_in_proj_kernel, feature_major_kv=kv_seq_len is not None, q_scale=q_scale),
        grid=(n // tm,),
        in_specs=in_specs,
        out_specs=out_specs,
        out_shape=out_shape,
        compiler_params=pltpu.CompilerParams(dimension_semantics=("arbitrary",), vmem_limit_bytes=VMEM_LIMIT),
        name="in_proj",
    )(x, w["gmix"], w["w_main"], w["w_small"], w["gmat"], w["qg"], w["kg"], w["sbias"], w["alog"])


def _cumsum_kernel(lf_ref, cum_ref, *, scale):
    n_chunks = lf_ref.shape[1] // LANES
    upper = (_iota((LANES, LANES), 0) <= _iota((LANES, LANES), 1)).astype(BF16)

    def body(c, carry):
        sl = pl.ds(pl.multiple_of(c * LANES, LANES), LANES)
        cs = carry + _dot_exact_rhs(lf_ref[:, sl], upper)
        cum_ref[:, sl] = cs * scale
        return jnp.broadcast_to(cs[:, LANES - 1:LANES], carry.shape)

    lax.fori_loop(0, n_chunks, body, jnp.zeros((FOX_HEADS, LANES), F32))


def _seq_cumsum(st, seq_len, scale):
    n = st.shape[1]
    spec = pl.BlockSpec((FOX_HEADS, seq_len), lambda b: (0, b))
    return pl.pallas_call(
        functools.partial(_cumsum_kernel, scale=scale),
        grid=(n // seq_len,),
        in_specs=[spec],
        out_specs=spec,
        out_shape=jax.ShapeDtypeStruct((FOX_HEADS, n), F32),
        compiler_params=pltpu.CompilerParams(dimension_semantics=("arbitrary",)),
        name="fox_cumsum",
    )(st)


def _fox_prompt_kernel(q_ref, k_ref, v_ref, ck_ref, o_ref, kmax_sc, *, tq):
    qi = pl.program_id(2)
    lane = _iota((tq, LANES), 1)
    first = lane < FOX_HEAD_DIM
    in_head = (first, jnp.logical_not(first))

    @pl.when(qi == 0)
    def _():
        def norms(c, carry):
            kk = k_ref[0, pl.ds(pl.multiple_of(c * tq, tq), tq), :].astype(F32)
            sq = kk * kk
            return tuple(jnp.maximum(carry[h], jnp.max(jnp.sum(jnp.where(in_head[h], sq, 0.0), axis=-1, keepdims=True),
                                                       axis=0, keepdims=True)) for h in range(2))
        zero = jnp.zeros((1, 1), F32)
        sq_max = lax.fori_loop(0, k_ref.shape[1] // tq, norms, (zero, zero))
        for h in range(2):
            kmax_sc[h] = jnp.broadcast_to(jnp.sqrt(sq_max[h]), kmax_sc.shape[1:])

    q2 = q_ref[0]
    causal = _iota((tq, tq), 0) >= _iota((tq, tq), 1)
    outs = []
    for h in range(2):
        qh = jnp.where(in_head[h], q2, jnp.zeros_like(q2))
        cq_row = ck_ref[0, h, :, pl.ds(pl.multiple_of(qi * tq, tq), tq)]
        cq = jnp.broadcast_to(cq_row, (LANES, tq)).T[:, 0:1]
        qf = qh.astype(F32)
        q_norm = jnp.sqrt(jnp.sum(qf * qf, axis=-1, keepdims=True))
        reach = q_norm * kmax_sc[h][0:1, 0:1] * NORM_SLACK + cq

        def step(j, carry, diagonal):
            m, l, acc = carry
            sl = pl.ds(pl.multiple_of(j * tq, tq), tq)
            s = (cq - ck_ref[0, h, :, sl]) + _dot_nt(qh, k_ref[0, sl, :])
            if diagonal:
                s = jnp.where(causal, s, -jnp.inf)
            m_new = jnp.maximum(m, jnp.max(s, axis=-1, keepdims=True))
            alpha = jnp.exp2(m - m_new)
            p = jnp.exp2(s - m_new)
            l = alpha * l + jnp.sum(p, axis=-1, keepdims=True)
            acc = alpha * acc + _dot(p.astype(BF16), v_ref[0, sl, :])
            return m_new, l, acc

        def reaches(j, m):
            sl = pl.ds(pl.multiple_of(jnp.maximum(j, 0) * tq, tq), tq)
            return jnp.max(reach - m) - jnp.min(ck_ref[0, h, :, sl]) >= -ZERO_PROB_LOG2

        def cond(c):
            return jnp.logical_and(c[0] >= 0, c[1])

        def body(c):
            j, _, m, l, acc = c
            m, l, acc = step(j, (m, l, acc), diagonal=False)
            return j - 1, reaches(j - 1, m), m, l, acc

        col = lambda val: jnp.full((tq, 1), val, F32)
        m, l, acc = step(qi, (col(NEG_BIG), col(0.0), jnp.zeros((tq, LANES), F32)), diagonal=True)
        _, _, m, l, acc = lax.while_loop(cond, body, (qi - 1, reaches(qi - 1, m), m, l, acc))
        outs.append(acc * (1.0 / l))
    o_ref[0] = jnp.where(first, outs[0], outs[1]).astype(o_ref.dtype)


def _fox_prompt(qb, kb, vb, ck, tq):
    b, seq_len, _ = qb.shape
    pairs = FOX_HEADS // 2
    return pl.pallas_call(
        functools.partial(_fox_prompt_kernel, tq=tq),
        grid=(b, pairs, seq_len // tq),
        in_specs=[
            pl.BlockSpec((1, tq, LANES), lambda b_, p, i: (b_, i, p)),
            pl.BlockSpec((1, seq_len, LANES), lambda b_, p, i: (b_, 0, p)),
            pl.BlockSpec((1, seq_len, LANES), lambda b_, p, i: (b_, 0, p)),
            pl.BlockSpec((1, 2, 1, seq_len), lambda b_, p, i: (b_, p, 0, 0)),
        ],
        out_specs=pl.BlockSpec((1, tq, LANES), lambda b_, p, i: (b_, i, p)),
        out_shape=jax.ShapeDtypeStruct((b, seq_len, FOX_WIDTH), BF16),
        scratch_shapes=[pltpu.VMEM((2, SUBLANES, LANES), F32)],
        compiler_params=pltpu.CompilerParams(dimension_semantics=("arbitrary", "arbitrary", "arbitrary"),
                                             vmem_limit_bytes=VMEM_LIMIT),
        name="fox_prompt_attention",
    )(qb, kb, vb, ck)


def _fox_sample_kernel(pt_ref, q_ref, *refs, pages_per_step, n_new):
    del pt_ref
    pp = pages_per_step
    k_refs, v_refs, lf_refs = refs[:pp], refs[pp:2 * pp], refs[2 * pp:3 * pp]
    knew_ref, vnew_ref, lfnew_ref, o_ref, m_sc, l_sc, acc_sc = refs[3 * pp:]
    step_id = pl.program_id(1)
    n_rows = n_new * FOX_HEADS

    @pl.when(step_id == 0)
    def _():
        m_sc[...] = jnp.full(m_sc.shape, NEG_BIG, F32)
        l_sc[...] = jnp.zeros(l_sc.shape, F32)
        acc_sc[...] = jnp.zeros(acc_sc.shape, F32)

    q = q_ref[0]
    ri = _iota((LANES, LANES), 0)
    ci = _iota((LANES, LANES), 1)
    after = (ri > ci).astype(BF16)
    upto = (ri <= ci).astype(BF16)
    tile_heads = lambda a: jnp.concatenate([a] * n_new, axis=0)

    def update(s, shift, v_t):
        m_prev = m_sc[...] + shift
        m_new = jnp.maximum(m_prev, jnp.max(s, axis=-1, keepdims=True))
        alpha = jnp.exp(m_prev - m_new)
        p = jnp.exp(s - m_new)
        l_sc[...] = alpha * l_sc[...] + jnp.sum(p, axis=-1, keepdims=True)
        acc_sc[...] = alpha * acc_sc[...] + _dot_nt(p.astype(BF16), v_t)
        m_sc[...] = m_new

    lf = jnp.concatenate([r[0] for r in lf_refs], axis=0)
    suffix = _dot_exact_rhs(lf, after)
    total = suffix[:, 0:1] + lf[:, 0:1]
    bias, later = [], jnp.zeros((FOX_HEADS, 1), F32)
    for j in reversed(range(pp)):
        rows = slice(j * FOX_HEADS, (j + 1) * FOX_HEADS)
        bias.append(tile_heads(suffix[rows] + later))
        later = later + total[rows]
    bias = jnp.concatenate(bias[::-1], axis=1)
    k_t = jnp.concatenate([r[0].astype(BF16) for r in k_refs], axis=1)
    v_t = jnp.concatenate([r[0].astype(BF16) for r in v_refs], axis=1)
    update(bias + _dot(q, k_t), tile_heads(later), v_t)

    @pl.when(step_id == pl.num_programs(1) - 1)
    def _():
        cum_rows = tile_heads(_dot_exact_rhs(lfnew_ref[0], upto))
        lane = _iota((n_rows, LANES), 1)
        query = _div_pow2(_iota((n_rows, LANES), 0), FOX_HEADS)
        cum_q = jnp.sum(jnp.where(lane == query, cum_rows, 0.0), axis=-1, keepdims=True)
        s = (cum_q - cum_rows) + _dot(q, knew_ref[0])
        s = jnp.where(lane <= query, s, -jnp.inf)
        update(s, cum_q, vnew_ref[0])
        out = acc_sc[...] / l_sc[...]
        own = _div_pow2(_iota(out.shape, 1), FOX_HEAD_DIM) == _mod_pow2(_iota(out.shape, 0), FOX_HEADS)
        out = jnp.where(own, out, 0.0)
        o_ref[0] = jnp.concatenate(
            [jnp.sum(out[i * FOX_HEADS:(i + 1) * FOX_HEADS], axis=0, keepdims=True) for i in range(n_new)], axis=0)


def _fox_sample(page_table, q_bd, k_pool_t, v_pool_t, lf_pool_t, k_new_t, v_new_t, lf_new_t, pages_per_step):
    n_seq, n_pages = page_table.shape
    n_rows = q_bd.shape[1]
    n_new = n_rows // FOX_HEADS
    page = k_pool_t.shape[2]
    pp = pages_per_step
    steps = n_pages // pp

    def paged(shape, j):
        return pl.BlockSpec(shape, lambda b, s, pt: (pt[b * n_pages + s * pp + j], 0, 0))

    per_seq = lambda shape: pl.BlockSpec(shape, lambda b, s, pt: (b, 0, 0))
    in_specs = ([per_seq((1, n_rows, FOX_WIDTH))]
                + [paged((1, FOX_WIDTH, page), j) for j in range(pp)]
                + [paged((1, FOX_WIDTH, page), j) for j in range(pp)]
                + [paged((1, FOX_HEADS, page), j) for j in range(pp)]
                + [per_seq((1, FOX_WIDTH, LANES)), per_seq((1, FOX_WIDTH, LANES)), per_seq((1, FOX_HEADS, LANES))])
    grid_spec = pltpu.PrefetchScalarGridSpec(
        num_scalar_prefetch=1,
        grid=(n_seq, steps),
        in_specs=in_specs,
        out_specs=per_seq((1, n_new, FOX_WIDTH)),
        scratch_shapes=[pltpu.VMEM((n_rows, 1), F32), pltpu.VMEM((n_rows, 1), F32),
                        pltpu.VMEM((n_rows, FOX_WIDTH), F32)],
    )
    return pl.pallas_call(
        functools.partial(_fox_sample_kernel, pages_per_step=pp, n_new=n_new),
        grid_spec=grid_spec,
        out_shape=jax.ShapeDtypeStruct((n_seq, n_new, FOX_WIDTH), F32),
        compiler_params=pltpu.CompilerParams(dimension_semantics=("arbitrary", "arbitrary"),
                                             vmem_limit_bytes=VMEM_LIMIT),
        name="fox_sample_attention",
    )(page_table.reshape(-1), q_bd, *([k_pool_t] * pp), *([v_pool_t] * pp), *([lf_pool_t] * pp),
      k_new_t, v_new_t, lf_new_t)


def _dot1(a, b):
    return _dot(a.astype(BF16), b.astype(BF16))


def _unit_lower_inverses(mats, same_block, chunk, base, dot):
    t_len = mats[0].shape[0]
    eye = (_iota((t_len, t_len), 0) == _iota((t_len, t_len), 1)).astype(F32)
    size = min(base, chunk)
    in_base = same_block(size)
    ns = [jnp.where(in_base, -a, 0.0) for a in mats]
    invs = [eye + n for n in ns]
    power = 1
    while 2 * power < size:
        ns = [dot(n, n) for n in ns]
        invs = [inv + dot(inv, n) for inv, n in zip(invs, ns)]
        power *= 2
    while size < chunk:
        off_mask = same_block(2 * size) & jnp.logical_not(same_block(size))
        corr = [dot(jnp.where(off_mask, a, 0.0), inv) for a, inv in zip(mats, invs)]
        invs = [inv - dot(inv, c) for inv, c in zip(invs, corr)]
        size *= 2
    return invs


def _short_conv(xbuf, conv_w, n_rows):
    first = SUBLANES - GDN_CONV_WIDTH + 1
    y = conv_w[0:1] * xbuf[first:first + n_rows, :]
    for i in range(1, GDN_CONV_WIDTH):
        y = y + conv_w[i:i + 1] * xbuf[first + i:first + i + n_rows, :]
    return y * _sigmoid(y)


class _ChunkMasks:
    def __init__(self, t_len, chunk):
        self.t_len, self.chunk = t_len, chunk
        ri = _iota((t_len, t_len), 0)
        ci = _iota((t_len, t_len), 1)
        self.same_block = lambda size: _div_pow2(ri, size) == _div_pow2(ci, size)
        in_chunk = self.same_block(chunk)
        self.lower = in_chunk & (ri >= ci)
        self.strict = in_chunk & (ri > ci)


def _chunk_cumsums(sblk, masks):
    chunk = masks.chunk
    cum_all = _dot_exact_lhs(masks.lower.astype(BF16), sblk)
    last_all = jnp.concatenate(
        [jnp.broadcast_to(cum_all[(c + 1) * chunk - 1:(c + 1) * chunk, :], (chunk, LANES))
         for c in range(masks.t_len // chunk)], axis=0)
    return cum_all, cum_all.T, last_all


def _delta_rule_chunk_terms(ybuf, sblks, cums, masks):
    sub = masks.t_len
    units = [(i, h) for i in range(len(sblks)) for h in range(GDN_HEADS)]
    col = lambda a, base, h: a[:, base + h:base + h + 1]
    betas = [col(sblks[i], ROW_BETA, h) for i, h in units]
    cumcols = [col(cums[i][0], ROW_G, h) for i, h in units]
    lasts = [col(cums[i][2], ROW_G, h) for i, h in units]
    decays = [jnp.where(masks.lower,
                        jnp.exp(jnp.where(masks.lower, cumcols[u] - cums[i][1][ROW_G + h:ROW_G + h + 1, :], 0.0)), 0.0)
              for u, (i, h) in enumerate(units)]
    qs, ks, vs = [], [], []
    for i, h in units:
        rows = slice(i * sub, (i + 1) * sub)
        q = ybuf[rows, h * GDN_KEY_DIM:(h + 1) * GDN_KEY_DIM]
        k = ybuf[rows, GDN_QK_WIDTH + h * GDN_KEY_DIM:GDN_QK_WIDTH + (h + 1) * GDN_KEY_DIM]
        qs.append(q * lax.rsqrt(jnp.sum(q * q, axis=-1, keepdims=True) + NORM_EPS) * (GDN_KEY_DIM ** -0.5))
        ks.append(k * lax.rsqrt(jnp.sum(k * k, axis=-1, keepdims=True) + NORM_EPS))
        vs.append(ybuf[rows, 2 * GDN_QK_WIDTH + h * GDN_VAL_DIM:2 * GDN_QK_WIDTH + (h + 1) * GDN_VAL_DIM])
    n = range(len(units))
    k_betas = [ks[u] * betas[u] for u in n]
    e_cums = [jnp.exp(cumcols[u]) for u in n]
    k_bfs = [ks[u].astype(BF16) for u in n]
    amats = [jnp.where(masks.strict, _dot_nt(k_betas[u].astype(BF16), k_bfs[u]) * decays[u], 0.0) for u in n]
    tmats = _unit_lower_inverses(amats, masks.same_block, masks.chunk, base=SUBLANES, dot=_dot1)
    uws = [_dot(tmats[u].astype(BF16),
                jnp.concatenate([vs[u] * betas[u], k_betas[u] * e_cums[u]], axis=-1).astype(BF16)) for u in n]
    qks = [(_dot_nt(qs[u].astype(BF16), k_bfs[u]) * decays[u]).astype(BF16) for u in n]
    flat = [(uws[u][:, :GDN_VAL_DIM], uws[u][:, GDN_VAL_DIM:].astype(BF16), qks[u], (qs[u] * e_cums[u]).astype(BF16),
             (ks[u] * jnp.exp(lasts[u] - cumcols[u])).astype(BF16), jnp.exp(lasts[u])) for u in n]
    return [flat[i * GDN_HEADS:(i + 1) * GDN_HEADS] for i in range(len(sblks))]


def _gated_out_norm(o, gz, gnorm):
    o = o * lax.rsqrt(jnp.mean(o * o, axis=-1, keepdims=True) + NORM_EPS) * gnorm
    return o * (gz * _sigmoid(gz))


def _gdn_prompt_kernel(gqkv_ref, s_ref, gz_ref, convw_ref, conv0_ref, s0_ref, gnorm_ref, o_ref, sfin_ref,
                       xbuf, ybuf, state, *, t_len, chunk):
    t = pl.program_id(1)

    @pl.when(t == 0)
    def _():
        xbuf[0:SUBLANES, :] = conv0_ref[0]
        state[...] = s0_ref[0]

    xbuf[SUBLANES:SUBLANES + t_len, :] = gqkv_ref[...]
    ybuf[...] = _short_conv(xbuf, convw_ref[...], t_len)
    xbuf[0:SUBLANES, :] = xbuf[t_len:t_len + SUBLANES, :]

    sub = min(t_len, LANES)
    masks = _ChunkMasks(sub, chunk)
    sblks = [s_ref[i * sub:(i + 1) * sub, :] for i in range(t_len // sub)]
    terms = _delta_rule_chunk_terms(ybuf, sblks, [_chunk_cumsums(sb, masks) for sb in sblks], masks)
    outs = [[] for _ in range(GDN_HEADS)]
    for sub_terms in terms:
        for c in range(sub // chunk):
            rows = slice(c * chunk, (c + 1) * chunk)
            for h, (u, wmat, qk, q_dec, k_dec, chunk_decay) in enumerate(sub_terms):
                st = state[h]
                ws = _dot(jnp.concatenate([wmat[rows], q_dec[rows]], axis=0), st.astype(BF16))
                v_new_bf = (u[rows] - ws[:chunk]).astype(BF16)
                outs[h].append(ws[chunk:] + _dot(qk[rows, rows], v_new_bf))
                state[h] = st * chunk_decay[c * chunk:c * chunk + 1] + _dot_tn(k_dec[rows], v_new_bf)
    for h in range(GDN_HEADS):
        head_cols = slice(h * GDN_VAL_DIM, (h + 1) * GDN_VAL_DIM)
        o_ref[:, head_cols] = _gated_out_norm(jnp.concatenate(outs[h], axis=0), gz_ref[:, head_cols],
                                              gnorm_ref[...]).astype(o_ref.dtype)

    @pl.when(t == pl.num_programs(1) - 1)
    def _():
        sfin_ref[0] = state[...]


def _gdn_prompt(gqkv, s, gz, conv_w, conv0, s0, gnorm, n_seq, t_len, chunk):
    n = gqkv.shape[0]
    tiles = n // n_seq // t_len
    rows = lambda width: pl.BlockSpec((t_len, width), lambda b, t: (b * tiles + t, 0))
    const = lambda shape: pl.BlockSpec(shape, lambda b, t: (0,) * len(shape))
    state_spec = pl.BlockSpec((1, GDN_HEADS, GDN_KEY_DIM, GDN_VAL_DIM), lambda b, t: (b, 0, 0, 0))
    return pl.pallas_call(
        functools.partial(_gdn_prompt_kernel, t_len=t_len, chunk=chunk),
        grid=(n_seq, tiles),
        in_specs=[rows(GDN_CONV_CH), rows(LANES), rows(GDN_V_WIDTH), const((GDN_CONV_WIDTH, GDN_CONV_CH)),
                  pl.BlockSpec((1, SUBLANES, GDN_CONV_CH), lambda b, t: (b, 0, 0)), state_spec,
                  const((1, GDN_VAL_DIM))],
        out_specs=(rows(GDN_V_WIDTH), state_spec),
        out_shape=(jax.ShapeDtypeStruct((n, GDN_V_WIDTH), BF16),
                   jax.ShapeDtypeStruct((n_seq, GDN_HEADS, GDN_KEY_DIM, GDN_VAL_DIM), F32)),
        scratch_shapes=[pltpu.VMEM((t_len + SUBLANES, GDN_CONV_CH), F32), pltpu.VMEM((t_len, GDN_CONV_CH), F32),
                        pltpu.VMEM((GDN_HEADS, GDN_KEY_DIM, GDN_VAL_DIM), F32)],
        compiler_params=pltpu.CompilerParams(dimension_semantics=("arbitrary", "arbitrary"),
                                             vmem_limit_bytes=VMEM_LIMIT),
        name="gated_deltanet_prompt",
    )(gqkv, s, gz, conv_w, conv0, s0, gnorm)


def _gdn_sample_kernel(xin_ref, s_ref, gz_ref, convw_ref, s0_ref, gnorm_ref, o_ref, sfin_ref, xbuf, ybuf,
                       *, n_seq, chunk):
    per_seq = SUBLANES + chunk
    n_in = n_seq * per_seq
    t_len = n_seq * chunk
    xbuf[0:SUBLANES, :] = jnp.zeros((SUBLANES, GDN_CONV_CH), F32)
    xbuf[SUBLANES:SUBLANES + n_in, :] = xin_ref[...]
    y = _short_conv(xbuf, convw_ref[...], n_in)
    for i in range(n_seq):
        ybuf[i * chunk:(i + 1) * chunk, :] = y[i * per_seq + SUBLANES:(i + 1) * per_seq]

    masks = _ChunkMasks(t_len, chunk)
    sblk = s_ref[...]
    cums = _chunk_cumsums(sblk, masks)
    seq_of_row = _div_pow2(_iota((t_len, GDN_KEY_DIM), 0), chunk)
    terms = _delta_rule_chunk_terms(ybuf, [sblk], [cums], masks)[0]
    for h, (u, wmat, qk, q_dec, k_dec, chunk_decay) in enumerate(terms):
        v_news, reads = [], []
        for i in range(n_seq):
            rows = slice(i * chunk, (i + 1) * chunk)
            ws = _dot(jnp.concatenate([wmat[rows], q_dec[rows]], axis=0), s0_ref[i, h].astype(BF16))
            v_news.append(u[rows] - ws[:chunk])
            reads.append(ws[chunk:])
        v_new_bf = jnp.concatenate(v_news, axis=0).astype(BF16)
        o = jnp.concatenate(reads, axis=0) + _dot(qk, v_new_bf)
        for i in range(n_seq):
            own_rows = jnp.where(seq_of_row == i, k_dec, jnp.zeros_like(k_dec))
            sfin_ref[i, h] = (s0_ref[i, h] * chunk_decay[i * chunk:i * chunk + 1]
                              + _dot_tn(own_rows, v_new_bf))
        head_cols = slice(h * GDN_VAL_DIM, (h + 1) * GDN_VAL_DIM)
        o_ref[:, head_cols] = _gated_out_norm(o, gz_ref[:, head_cols], gnorm_ref[...])


def _gdn_sample(xin, s, gz, conv_w, s0, gnorm, n_seq_total, seq_per_step, chunk):
    steps = n_seq_total // seq_per_step
    per_seq = SUBLANES + chunk
    t_len = seq_per_step * chunk
    rows = lambda width: pl.BlockSpec((t_len, width), lambda i: (i, 0))
    const = lambda shape: pl.BlockSpec(shape, lambda i: (0,) * len(shape))
    state_spec = pl.BlockSpec((seq_per_step, GDN_HEADS, GDN_KEY_DIM, GDN_VAL_DIM), lambda i: (i, 0, 0, 0))
    return pl.pallas_call(
        functools.partial(_gdn_sample_kernel, n_seq=seq_per_step, chunk=chunk),
        grid=(steps,),
        in_specs=[pl.BlockSpec((seq_per_step * per_seq, GDN_CONV_CH), lambda i: (i, 0)), rows(LANES),
                  rows(GDN_V_WIDTH), const((GDN_CONV_WIDTH, GDN_CONV_CH)), state_spec, const((1, GDN_VAL_DIM))],
        out_specs=(rows(GDN_V_WIDTH), state_spec),
        out_shape=(jax.ShapeDtypeStruct((n_seq_total * chunk, GDN_V_WIDTH), F32),
                   jax.ShapeDtypeStruct((n_seq_total, GDN_HEADS, GDN_KEY_DIM, GDN_VAL_DIM), F32)),
        scratch_shapes=[pltpu.VMEM((seq_per_step * per_seq + SUBLANES, GDN_CONV_CH), F32),
                        pltpu.VMEM((t_len, GDN_CONV_CH), F32)],
        compiler_params=pltpu.CompilerParams(dimension_semantics=("arbitrary",), vmem_limit_bytes=VMEM_LIMIT),
        name="gated_deltanet_sample",
    )(xin, s, gz, conv_w, s0, gnorm)


def _post_kernel(x_ref, oa_ref, ob_ref, gates_ref, ple_ref, wa_ref, wb_ref, wout_ref, gmlp_ref, wup_ref,
                 wdown_ref, gple_ref, wpg_ref, wple_ref, y_ref):
    def rms(a, g):
        return a * lax.rsqrt(jnp.mean(a * a, axis=-1, keepdims=True) + NORM_EPS) * g

    tm = x_ref.shape[0]
    n_sub = 1
    subs = [slice(i * (tm // n_sub), (i + 1) * (tm // n_sub)) for i in range(n_sub)]
    gates = [_sigmoid(gates_ref[r, :].astype(F32)) for r in subs]
    a = [_dot(oa_ref[r, :].astype(BF16), wa_ref[...]) for r in subs]
    b = [_dot(ob_ref[r, :].astype(BF16), wb_ref[...]) for r in subs]
    merged = [(g[:, :D_MODEL] * ai + g[:, D_MODEL:] * bi).astype(BF16) for g, ai, bi in zip(gates, a, b)]
    x = [x_ref[r, :] + _dot(m, wout_ref[...]) for r, m in zip(subs, merged)]
    h = [rms(xi, gmlp_ref[...]).astype(BF16) for xi in x]
    up = [jnp.maximum(_dot(hi, wup_ref[...]), 0.0) for hi in h]
    x = [xi + _dot((u * u).astype(BF16), wdown_ref[...]) for xi, u in zip(x, up)]
    hp = [rms(xi, gple_ref[...]).astype(BF16) for xi in x]
    ple_gate = [_sigmoid(_dot(hi, wpg_ref[...])) for hi in hp]
    for r, xi, g in zip(subs, x, ple_gate):
        y_ref[r, :] = xi + g * _dot(ple_ref[r, :].astype(BF16), wple_ref[...])


def _post(x, oa, ob, gates, ple, w, tm):
    n = x.shape[0]
    rows = lambda width: pl.BlockSpec((tm, width), lambda i: (i, 0))
    resident = lambda shape: pl.BlockSpec(shape, lambda i: (0, 0), pipeline_mode=pl.Buffered(1))
    return pl.pallas_call(
        _post_kernel,
        grid=(n // tm,),
        in_specs=[rows(D_MODEL), rows(FOX_WIDTH), rows(GDN_V_WIDTH), rows(2 * D_MODEL), rows(PLE_DIM),
                  resident((FOX_WIDTH, D_MODEL)), resident((GDN_V_WIDTH, D_MODEL)), resident((D_MODEL, D_MODEL)),
                  resident((1, D_MODEL)), resident((D_MODEL, D_FF)), resident((D_FF, D_MODEL)),
                  resident((1, D_MODEL)), resident((D_MODEL, D_MODEL)), resident((PLE_DIM, D_MODEL))],
        out_specs=rows(D_MODEL),
        out_shape=jax.ShapeDtypeStruct((n, D_MODEL), F32),
        compiler_params=pltpu.CompilerParams(dimension_semantics=("arbitrary",), vmem_limit_bytes=VMEM_LIMIT),
        name="merge_mlp_ple",
    )(x, oa, ob, gates, ple, w["w_a"], w["w_b"], w["w_out"], w["gmlp"], w["w_up"], w["w_down"], w["gple"],
      w["w_pg"], w["w_ple"])


def _prepare_weights(l, norm_mix_g, w_in, fox_f_bias, fox_q_norm_g, fox_k_norm_g, gdn_conv_w, gdn_a_log, gdn_dt_bias,
                     gdn_out_norm_g, w_branch_a, w_branch_b, w_out, norm_mlp_g, w_up, w_down, norm_ple_g,
                     w_ple_gate, w_ple):
    wi = w_in[l]
    o_ff = 3 * FOX_WIDTH
    o_gqkv = o_ff + FOX_HEADS
    o_ga = o_gqkv + GDN_CONV_CH
    o_gb = o_ga + GDN_HEADS
    o_gz = o_gb + GDN_HEADS
    o_gates = o_gz + GDN_V_WIDTH
    w_main = jnp.concatenate([wi[:, :o_ff], wi[:, o_gqkv:o_ga], wi[:, o_gz:]], axis=1).astype(BF16)
    w_small = jnp.concatenate([wi[:, o_ff:o_gqkv], wi[:, o_ga:o_gz]], axis=1)
    w_small = jnp.pad(w_small, ((0, 0), (0, LANES - N_SMALL))).astype(BF16)
    pad_col = lambda parts: jnp.pad(jnp.concatenate(parts), (0, LANES - N_SMALL)).reshape(LANES, 1).astype(F32)
    zeros_h = jnp.zeros((GDN_HEADS,), F32)
    head = jnp.arange(FOX_WIDTH) // FOX_HEAD_DIM
    return dict(
        gmix=norm_mix_g[l].reshape(1, D_MODEL),
        w_main=w_main,
        w_small=w_small,
        gmat=((head[:, None] == head[None, :]).astype(F32) / FOX_HEAD_DIM).astype(BF16),
        qg=jnp.tile(fox_q_norm_g[l], FOX_HEADS).reshape(1, FOX_WIDTH),
        kg=jnp.tile(fox_k_norm_g[l], FOX_HEADS).reshape(1, FOX_WIDTH),
        sbias=pad_col([fox_f_bias[l], gdn_dt_bias[l], zeros_h]),
        alog=pad_col([jnp.zeros((FOX_HEADS,), F32), gdn_a_log[l], zeros_h]),
        conv_w=gdn_conv_w[l],
        gnorm=gdn_out_norm_g[l].reshape(1, GDN_VAL_DIM),
        w_a=w_branch_a[l].astype(BF16),
        w_b=w_branch_b[l].astype(BF16),
        w_out=w_out[l].astype(BF16),
        gmlp=norm_mlp_g[l].reshape(1, D_MODEL),
        w_up=w_up[l].astype(BF16),
        w_down=w_down[l].astype(BF16),
        gple=norm_ple_g[l].reshape(1, D_MODEL),
        w_pg=w_ple_gate[l].astype(BF16),
        w_ple=w_ple[l].astype(BF16),
    )


def _pick_tile(n, target):
    t = min(n, target)
    while n % t:
        t //= 2
    return t


def _prompt_layer(x, ple, w):
    b, seq_len, _ = x.shape
    n = b * seq_len
    tm = _pick_tile(n, 256)
    q_bf, k_t, k_bf, v_t, v_bf, gqkv, gz, gates, s, st = _in_proj(
        x.reshape(n, D_MODEL), w, tm, FOX_HEAD_DIM ** -0.5 * LOG2E, kv_seq_len=seq_len)

    cum = _seq_cumsum(st, seq_len, LOG2E)
    ck = cum.reshape(FOX_HEADS, b, 1, seq_len).transpose(1, 0, 2, 3)
    tq = _pick_tile(seq_len, 512)
    o_a = _fox_prompt(q_bf.reshape(b, seq_len, FOX_WIDTH), k_bf.reshape(b, seq_len, FOX_WIDTH),
                      v_bf.reshape(b, seq_len, FOX_WIDTH), ck, tq)

    chunk = math.gcd(seq_len, GDN_CHUNK)
    t_len = _pick_tile(seq_len, 8 * chunk)
    conv0 = jnp.zeros((b, SUBLANES, GDN_CONV_CH), F32)
    ssm0 = jnp.zeros((b, GDN_HEADS, GDN_KEY_DIM, GDN_VAL_DIM), F32)
    o_b, ssm = _gdn_prompt(gqkv, s, gz, w["conv_w"], conv0, ssm0, w["gnorm"], b, t_len, chunk)

    y = _post(x.reshape(n, D_MODEL), o_a.reshape(n, FOX_WIDTH), o_b, gates, ple.reshape(n, PLE_DIM), w,
              _pick_tile(n, 512))
    keep = GDN_CONV_WIDTH - 1
    token_major = lambda a: a.reshape(b, FOX_HEADS, FOX_HEAD_DIM, seq_len).transpose(0, 3, 1, 2)
    states = (token_major(k_t), token_major(v_t),
              st[ROW_LOGF:ROW_LOGF + FOX_HEADS].reshape(FOX_HEADS, b, seq_len).transpose(1, 2, 0),
              gqkv.reshape(b, seq_len, GDN_CONV_CH)[:, seq_len - keep:], ssm)
    return y.reshape(b, seq_len, D_MODEL), states


def _sample_layer(x, ple, w, k_pool, v_pool, lf_pool, conv_buf, ssm_state, page_table):
    b, s_new, _ = x.shape
    n = b * s_new
    keep = GDN_CONV_WIDTH - 1
    assert s_new >= keep and s_new <= SUBLANES
    tm = _pick_tile(n, 256)
    q_bf, k, k_bf, v, v_bf, gqkv, gz, gates, s, st = _in_proj(x.reshape(n, D_MODEL), w, tm, FOX_HEAD_DIM ** -0.5)

    n_pool, page = k_pool.shape[:2]
    q4 = q_bf.reshape(b, s_new, FOX_HEADS, 1, FOX_HEAD_DIM)
    eye = jnp.eye(FOX_HEADS, dtype=BF16).reshape(1, 1, FOX_HEADS, FOX_HEADS, 1)
    q_bd = (q4 * eye).reshape(b, s_new * FOX_HEADS, FOX_WIDTH)
    pad_keys = lambda a: jnp.pad(a.reshape(b, s_new, FOX_WIDTH).transpose(0, 2, 1),
                                 ((0, 0), (0, 0), (0, LANES - s_new)))
    lf_new_t = jnp.pad(st[ROW_LOGF:ROW_LOGF + FOX_HEADS].reshape(FOX_HEADS, b, s_new).transpose(1, 0, 2),
                       ((0, 0), (0, 0), (0, LANES - s_new)))
    pool_t = lambda a: a.transpose(0, 2, 3, 1).reshape(n_pool, FOX_WIDTH, page)
    o_a = _fox_sample(page_table, q_bd, pool_t(k_pool), pool_t(v_pool), lf_pool.transpose(0, 2, 1),
                      pad_keys(k_bf), pad_keys(v_bf), lf_new_t, pages_per_step=_pick_tile(page_table.shape[1], 32))

    chunk = SUBLANES
    pad_tok = lambda a: jnp.pad(a.reshape(b, s_new, -1), ((0, 0), (0, chunk - s_new), (0, 0)))
    xin = jnp.concatenate([jnp.pad(conv_buf, ((0, 0), (SUBLANES - keep, 0), (0, 0))), pad_tok(gqkv)], axis=1)
    seq_per_step = _pick_tile(b, LANES // chunk)
    o_b, ssm = _gdn_sample(xin.reshape(b * (SUBLANES + chunk), GDN_CONV_CH), pad_tok(s).reshape(b * chunk, LANES),
                           pad_tok(gz).reshape(b * chunk, GDN_V_WIDTH), w["conv_w"], ssm_state, w["gnorm"],
                           b, seq_per_step, chunk)
    o_b = o_b.reshape(b, chunk, GDN_V_WIDTH)[:, :s_new].reshape(n, GDN_V_WIDTH)

    y = _post(x.reshape(n, D_MODEL), o_a.reshape(n, FOX_WIDTH), o_b, gates, ple.reshape(n, PLE_DIM), w, tm)
    states = (k.reshape(b, s_new, FOX_HEADS, FOX_HEAD_DIM), v.reshape(b, s_new, FOX_HEADS, FOX_HEAD_DIM),
              s[:, ROW_LOGF:ROW_LOGF + FOX_HEADS].reshape(b, s_new, FOX_HEADS),
              gqkv.reshape(b, s_new, GDN_CONV_CH)[:, s_new - keep:], ssm)
    return y.reshape(b, s_new, D_MODEL), states


def kernel(x_prompt, x_sample, p_prompt, p_sample, cache_k, cache_v, cache_logf, state_conv, state_ssm, page_table,
           norm_mix_g, w_in, fox_f_bias, fox_q_norm_g, fox_k_norm_g, gdn_conv_w, gdn_a_log, gdn_dt_bias,
           gdn_out_norm_g, w_branch_a, w_branch_b, w_out, norm_mlp_g, w_up, w_down, norm_ple_g, w_ple_gate, w_ple):
    depth = w_in.shape[0]
    y_prompt, y_sample = x_prompt, x_sample
    prompt_states, sample_states = [], []
    for l in range(depth):
        w = _prepare_weights(l, norm_mix_g, w_in, fox_f_bias, fox_q_norm_g, fox_k_norm_g, gdn_conv_w, gdn_a_log,
                             gdn_dt_bias, gdn_out_norm_g, w_branch_a, w_branch_b, w_out, norm_mlp_g, w_up, w_down,
                             norm_ple_g, w_ple_gate, w_ple)
        y_prompt, st_p = _prompt_layer(y_prompt, p_prompt[l], w)
        y_sample, st_s = _sample_layer(y_sample, p_sample[l], w, cache_k[l], cache_v[l], cache_logf[l],
                                       state_conv[l], state_ssm[l], page_table)
        prompt_states.append(st_p)
        sample_states.append(st_s)
    stack = lambda states, i: jnp.stack([st[i] for st in states])
    return ((y_prompt, y_sample) + tuple(stack(prompt_states, i) for i in range(5))
            + tuple(stack(sample_states, i) for i in range(5)))
```

```python
import functools
import math

import jax
import jax.numpy as jnp
from jax import lax
from jax.experimental import pallas as pl
from jax.experimental.pallas import tpu as pltpu

F32 = jnp.float32
BF16 = jnp.bfloat16

D_MODEL = 1024
FOX_HEADS = 8
FOX_HEAD_DIM = 64
FOX_WIDTH = FOX_HEADS * FOX_HEAD_DIM
GDN_HEADS = 4
GDN_KEY_DIM = 128
GDN_VAL_DIM = 128
GDN_QK_WIDTH = GDN_HEADS * GDN_KEY_DIM
GDN_V_WIDTH = GDN_HEADS * GDN_VAL_DIM
GDN_CONV_WIDTH = 4
GDN_CONV_CH = 2 * GDN_QK_WIDTH + GDN_V_WIDTH
GDN_CHUNK = 64
D_FF = 4 * D_MODEL
PLE_DIM = 256
NORM_EPS = 1e-6

LANES = 128
SUBLANES = 8
NEG_BIG = -1e30
LOG2E = 1.4426950408889634
ZERO_PROB_LOG2 = 152.0
NORM_SLACK = 1.0 + 2.0 ** -7
VMEM_LIMIT = 56 * 1024 * 1024

_MAIN_SEGS = (FOX_WIDTH, FOX_WIDTH, FOX_WIDTH, GDN_CONV_CH, GDN_V_WIDTH, 2 * D_MODEL)
_MAIN_OFFS = tuple(sum(_MAIN_SEGS[:i]) for i in range(len(_MAIN_SEGS) + 1))
D_MAIN = _MAIN_OFFS[-1]
ROW_LOGF = 0
ROW_G = FOX_HEADS
ROW_BETA = FOX_HEADS + GDN_HEADS
N_SMALL = FOX_HEADS + 2 * GDN_HEADS


def _dot(a, b):
    return jnp.dot(a, b, preferred_element_type=F32)


def _dot_nt(a, b):
    return lax.dot_general(a, b, (((1,), (1,)), ((), ())), preferred_element_type=F32)


def _dot_tn(a, b):
    return lax.dot_general(a, b, (((0,), (0,)), ((), ())), preferred_element_type=F32)


def _split(a, parts, axis):
    pieces = []
    for _ in range(parts - 1):
        p = a.astype(BF16).astype(F32)
        pieces.append(p)
        a = a - p
    pieces.append(a)
    return jnp.concatenate(pieces, axis=axis).astype(BF16)


def _dot3(a, b):
    m, n = a.shape[0], b.shape[1]
    r = _dot(_split(a, 2, 0), _split(b, 2, 1))
    return (r[:m, :n] + r[:m, n:]) + (r[m:, :n] + r[m:, n:])


def _dot_exact_lhs(a_bf, b):
    n = b.shape[1]
    r = _dot(a_bf, _split(b, 3, 1))
    return r[:, :n] + (r[:, n:2 * n] + r[:, 2 * n:])


def _dot_exact_rhs(a, b_bf):
    m = a.shape[0]
    r = _dot(_split(a, 3, 0), b_bf)
    return r[:m] + (r[m:2 * m] + r[2 * m:])


def _sigmoid(x):
    return 1.0 / (1.0 + jnp.exp(-x))


def _iota(shape, dim):
    return lax.broadcasted_iota(jnp.int32, shape, dim)


def _div_pow2(x, divisor):
    shift = divisor.bit_length() - 1
    assert divisor == 1 << shift
    return lax.shift_right_logical(x, shift)


def _mod_pow2(x, divisor):
    assert divisor & (divisor - 1) == 0
    return x & (divisor - 1)


def _in_proj_kernel(x_ref, gmix_ref, wmain_ref, wsmall_ref, gmat_ref, qg_ref, kg_ref, sbias_ref, alog_ref,
                    q_ref, k_ref, kb_ref, v_ref, vb_ref, gqkv_ref, gz_ref, gates_ref, s_ref, st_ref,
                    *, feature_major_kv, q_scale):
    x = x_ref[...]
    xn = x * lax.rsqrt(jnp.mean(x * x, axis=-1, keepdims=True) + NORM_EPS) * gmix_ref[...]
    xb = xn.astype(BF16)

    def proj(seg):
        return _dot(xb, wmain_ref[:, _MAIN_OFFS[seg]:_MAIN_OFFS[seg + 1]])

    gmat = gmat_ref[...]

    def head_norm(t, g):
        ms = _dot((t * t).astype(BF16), gmat)
        return t * lax.rsqrt(ms + NORM_EPS) * g

    q = head_norm(proj(0), qg_ref[...])
    q_ref[...] = (q * q_scale).astype(BF16)
    k = head_norm(proj(1), kg_ref[...])
    kb_ref[...] = k.astype(BF16)
    v = proj(2)
    vb_ref[...] = v.astype(BF16)
    if feature_major_kv:
        k_ref[0] = k.T
        v_ref[0] = v.T
    else:
        k_ref[...] = k
        v_ref[...] = v
    gqkv_ref[...] = proj(3)
    gz_ref[...] = proj(4)
    gates_ref[...] = proj(5).astype(gates_ref.dtype)

    z = _dot(xb, wsmall_ref[...]).T + sbias_ref[...]
    row = _iota(z.shape, 0)
    t = jnp.log1p(jnp.exp(-jnp.abs(z)))
    logf = jnp.minimum(z, 0.0) - t
    g = -jnp.exp(alog_ref[...]) * (jnp.maximum(z, 0.0) + t)
    beta = _sigmoid(z)
    res = jnp.where(row < ROW_G, logf, jnp.where(row < ROW_BETA, g, jnp.where(row < N_SMALL, beta, 0.0)))
    st_ref[...] = res[:N_SMALL]
    s_ref[...] = res.T


def _in_proj(x, w, tm, q_scale, kv_seq_len=None):
    n = x.shape[0]
    const = lambda i: (0, 0)
    rows = lambda width: pl.BlockSpec((tm, width), lambda i: (i, 0))
    resident = lambda shape: pl.BlockSpec(shape, const, pipeline_mode=pl.Buffered(1))
    if kv_seq_len is None:
        kv_shape, kv_spec = jax.ShapeDtypeStruct((n, FOX_WIDTH), F32), rows(FOX_WIDTH)
    else:
        tiles = kv_seq_len // tm
        kv_shape = jax.ShapeDtypeStruct((n // kv_seq_len, FOX_WIDTH, kv_seq_len), F32)
        kv_spec = pl.BlockSpec((1, FOX_WIDTH, tm), lambda i: (i // tiles, 0, i % tiles))
    out_shape = (
        jax.ShapeDtypeStruct((n, FOX_WIDTH), BF16),
        kv_shape,
        jax.ShapeDtypeStruct((n, FOX_WIDTH), BF16),
        kv_shape,
        jax.ShapeDtypeStruct((n, FOX_WIDTH), BF16),
        jax.ShapeDtypeStruct((n, GDN_CONV_CH), F32),
        jax.ShapeDtypeStruct((n, GDN_V_WIDTH), F32),
        jax.ShapeDtypeStruct((n, 2 * D_MODEL), BF16),
        jax.ShapeDtypeStruct((n, LANES), F32),
        jax.ShapeDtypeStruct((N_SMALL, n), F32),
    )
    out_specs = (
        rows(FOX_WIDTH), kv_spec, rows(FOX_WIDTH), kv_spec, rows(FOX_WIDTH),
        rows(GDN_CONV_CH), rows(GDN_V_WIDTH), rows(2 * D_MODEL), rows(LANES),
        pl.BlockSpec((N_SMALL, tm), lambda i: (0, i)),
    )
    in_specs = [
        rows(D_MODEL),
        resident((1, D_MODEL)),
        resident((D_MODEL, D_MAIN)),
        resident((D_MODEL, LANES)),
        resident((FOX_WIDTH, FOX_WIDTH)),
        resident((1, FOX_WIDTH)),
        resident((1, FOX_WIDTH)),
        resident((LANES, 1)),
        resident((LANES, 1)),
    ]
    return pl.pallas_call(
        functools.partial(_in_proj_kernel, feature_major_kv=kv_seq_len is not None, q_scale=q_scale),
        grid=(n // tm,),
        in_specs=in_specs,
        out_specs=out_specs,
        out_shape=out_shape,
        compiler_params=pltpu.CompilerParams(dimension_semantics=("arbitrary",), vmem_limit_bytes=VMEM_LIMIT),
        name="in_proj",
    )(x, w["gmix"], w["w_main"], w["w_small"], w["gmat"], w["qg"], w["kg"], w["sbias"], w["alog"])


def _cumsum_kernel(lf_ref, cum_ref, *, scale):
    n_chunks = lf_ref.shape[1] // LANES
    upper = (_iota((LANES, LANES), 0) <= _iota((LANES, LANES), 1)).astype(BF16)
    chunks = jnp.concatenate([lf_ref[:, c * LANES:(c + 1) * LANES] for c in range(n_chunks)], axis=0)
    local = _dot_exact_rhs(chunks, upper)
    carry = jnp.zeros((FOX_HEADS, 1), F32)
    for c in range(n_chunks):
        cs = carry + local[c * FOX_HEADS:(c + 1) * FOX_HEADS]
        cum_ref[:, c * LANES:(c + 1) * LANES] = cs * scale
        carry = cs[:, LANES - 1:LANES]


def _seq_cumsum(st, seq_len, scale):
    n = st.shape[1]
    spec = pl.BlockSpec((FOX_HEADS, seq_len), lambda b: (0, b))
    return pl.pallas_call(
        functools.partial(_cumsum_kernel, scale=scale),
        grid=(n // seq_len,),
        in_specs=[spec],
        out_specs=spec,
        out_shape=jax.ShapeDtypeStruct((FOX_HEADS, n), F32),
        compiler_params=pltpu.CompilerParams(dimension_semantics=("arbitrary",)),
        name="fox_cumsum",
    )(st)


def _fox_prompt_kernel(q_ref, k_ref, v_ref, ck_ref, o_ref, kmax_sc, ckmin_sc, *, tq):
    qi = pl.program_id(2)
    lane = _iota((tq, LANES), 1)
    first = lane < FOX_HEAD_DIM
    in_head = (first, jnp.logical_not(first))
    tile_lane = _iota((1, LANES), 1)
    heads = (0, 1)

    @pl.when(qi == 0)
    def _():
        def scan(c, carry):
            sq_max, ck_min = carry
            sl = pl.ds(pl.multiple_of(c * tq, tq), tq)
            kk = k_ref[0, sl, :].astype(F32)
            sq = kk * kk
            sq_max = tuple(jnp.maximum(sq_max[h], jnp.max(jnp.sum(jnp.where(in_head[h], sq, 0.0), axis=-1,
                                                                  keepdims=True), axis=0, keepdims=True))
                           for h in heads)
            ck_min = tuple(jnp.where(tile_lane == c, jnp.min(ck_ref[0, h, :, sl], axis=-1, keepdims=True), ck_min[h])
                           for h in heads)
            return sq_max, ck_min
        zero, zero_row = jnp.zeros((1, 1), F32), jnp.zeros((1, LANES), F32)
        sq_max, ck_min = lax.fori_loop(0, k_ref.shape[1] // tq, scan, ((zero, zero), (zero_row, zero_row)))
        for h in heads:
            kmax_sc[h] = jnp.broadcast_to(jnp.sqrt(sq_max[h]), kmax_sc.shape[1:])
            ckmin_sc[h] = jnp.broadcast_to(ck_min[h], ckmin_sc.shape[1:])

    q2 = q_ref[0]
    causal = _iota((tq, tq), 0) >= _iota((tq, tq), 1)
    qs, cqs, reach = [], [], []
    for h in heads:
        qh = jnp.where(in_head[h], q2, jnp.zeros_like(q2))
        cq_row = ck_ref[0, h, :, pl.ds(pl.multiple_of(qi * tq, tq), tq)]
        cq = jnp.broadcast_to(cq_row, (LANES, tq)).T[:, 0:1]
        qf = qh.astype(F32)
        q_norm = jnp.sqrt(jnp.sum(qf * qf, axis=-1, keepdims=True))
        qs.append(qh)
        cqs.append(cq)
        reach.append(q_norm * kmax_sc[h][0:1, 0:1] * NORM_SLACK + cq)

    def steps(hs, j, carries, diagonal):
        sl = pl.ds(pl.multiple_of(j * tq, tq), tq)
        k2 = k_ref[0, sl, :]
        v2 = v_ref[0, sl, :]
        ss = [(cqs[h] - ck_ref[0, h, :, sl]) + _dot_nt(qs[h], k2) for h in hs]
        if diagonal:
            ss = [jnp.where(causal, s, -jnp.inf) for s in ss]
        m_news = [jnp.maximum(c[0], jnp.max(s, axis=-1, keepdims=True)) for c, s in zip(carries, ss)]
        alphas = [jnp.exp2(c[0] - m_new) for c, m_new in zip(carries, m_news)]
        ps = [jnp.exp2(s - m_new) for s, m_new in zip(ss, m_news)]
        ls = [alpha * c[1] + jnp.sum(p, axis=-1, keepdims=True) for alpha, c, p in zip(alphas, carries, ps)]
        accs = [alpha * c[2] + _dot(p.astype(BF16), v2) for alpha, c, p in zip(alphas, carries, ps)]
        return tuple(zip(m_news, ls, accs))

    def tiles_reached(h, m):
        gap = jnp.max(reach[h] - m, axis=0, keepdims=True) - ckmin_sc[h][0:1, :]
        hit = jnp.logical_and(tile_lane < qi, gap >= -ZERO_PROB_LOG2)
        return jnp.sum(hit.astype(F32)).astype(jnp.int32)

    col = lambda val: jnp.full((tq, 1), val, F32)
    init = (col(NEG_BIG), col(0.0), jnp.zeros((tq, LANES), F32))
    ca, cb = steps(heads, qi, (init, init), diagonal=True)
    n_a, n_b = tiles_reached(0, ca[0]), tiles_reached(1, cb[0])
    n_both = jnp.minimum(n_a, n_b)
    ca, cb = lax.fori_loop(0, n_both, lambda i, c: steps(heads, qi - 1 - i, c, diagonal=False), (ca, cb))
    ca = lax.fori_loop(n_both, n_a, lambda i, c: steps((0,), qi - 1 - i, (c,), diagonal=False)[0], ca)
    cb = lax.fori_loop(n_both, n_b, lambda i, c: steps((1,), qi - 1 - i, (c,), diagonal=False)[0], cb)
    o_ref[0] = jnp.where(first, ca[2] * (1.0 / ca[1]), cb[2] * (1.0 / cb[1])).astype(o_ref.dtype)


def _fox_prompt(qb, kb, vb, ck, tq):
    b, seq_len, _ = qb.shape
    pairs = FOX_HEADS // 2
    return pl.pallas_call(
        functools.partial(_fox_prompt_kernel, tq=tq),
        grid=(b, pairs, seq_len // tq),
        in_specs=[
            pl.BlockSpec((1, tq, LANES), lambda b_, p, i: (b_, i, p)),
            pl.BlockSpec((1, seq_len, LANES), lambda b_, p, i: (b_, 0, p)),
            pl.BlockSpec((1, seq_len, LANES), lambda b_, p, i: (b_, 0, p)),
            pl.BlockSpec((1, 2, 1, seq_len), lambda b_, p, i: (b_, p, 0, 0)),
        ],
        out_specs=pl.BlockSpec((1, tq, LANES), lambda b_, p, i: (b_, i, p)),
        out_shape=jax.ShapeDtypeStruct((b, seq_len, FOX_WIDTH), BF16),
        scratch_shapes=[pltpu.VMEM((2, SUBLANES, LANES), F32), pltpu.VMEM((2, SUBLANES, LANES), F32)],
        compiler_params=pltpu.CompilerParams(dimension_semantics=("arbitrary", "arbitrary", "arbitrary"),
                                             vmem_limit_bytes=VMEM_LIMIT),
        name="fox_prompt_attention",
    )(qb, kb, vb, ck)


def _fox_sample_kernel(pt_ref, q_ref, *refs, pages_per_step, n_new):
    del pt_ref
    pp = pages_per_step
    k_refs, v_refs, lf_refs = refs[:pp], refs[pp:2 * pp], refs[2 * pp:3 * pp]
    knew_ref, vnew_ref, lfnew_ref, o_ref, m_sc, l_sc, acc_sc = refs[3 * pp:]
    step_id = pl.program_id(1)
    n_rows = n_new * FOX_HEADS

    @pl.when(step_id == 0)
    def _():
        m_sc[...] = jnp.full(m_sc.shape, NEG_BIG, F32)
        l_sc[...] = jnp.zeros(l_sc.shape, F32)
        acc_sc[...] = jnp.zeros(acc_sc.shape, F32)

    q = q_ref[0]
    ri = _iota((LANES, LANES), 0)
    ci = _iota((LANES, LANES), 1)
    after = (ri > ci).astype(BF16)
    upto = (ri <= ci).astype(BF16)
    tile_heads = lambda a: jnp.concatenate([a] * n_new, axis=0)

    def update(s, shift, v_t):
        m_prev = m_sc[...] + shift
        m_new = jnp.maximum(m_prev, jnp.max(s, axis=-1, keepdims=True))
        alpha = jnp.exp(m_prev - m_new)
        p = jnp.exp(s - m_new)
        l_sc[...] = alpha * l_sc[...] + jnp.sum(p, axis=-1, keepdims=True)
        acc_sc[...] = alpha * acc_sc[...] + _dot_nt(p.astype(BF16), v_t)
        m_sc[...] = m_new

    lf = jnp.concatenate([r[0] for r in lf_refs], axis=0)
    suffix = _dot_exact_rhs(lf, after)
    total = suffix[:, 0:1] + lf[:, 0:1]
    bias, later = [], jnp.zeros((FOX_HEADS, 1), F32)
    for j in reversed(range(pp)):
        rows = slice(j * FOX_HEADS, (j + 1) * FOX_HEADS)
        bias.append(tile_heads(suffix[rows] + later))
        later = later + total[rows]
    bias = jnp.concatenate(bias[::-1], axis=1)
    k_t = jnp.concatenate([r[0].astype(BF16) for r in k_refs], axis=1)
    v_t = jnp.concatenate([r[0].astype(BF16) for r in v_refs], axis=1)
    update(bias + _dot(q, k_t), tile_heads(later), v_t)

    @pl.when(step_id == pl.num_programs(1) - 1)
    def _():
        cum_rows = tile_heads(_dot_exact_rhs(lfnew_ref[0], upto))
        lane = _iota((n_rows, LANES), 1)
        query = _div_pow2(_iota((n_rows, LANES), 0), FOX_HEADS)
        cum_q = jnp.sum(jnp.where(lane == query, cum_rows, 0.0), axis=-1, keepdims=True)
        s = (cum_q - cum_rows) + _dot(q, knew_ref[0])
        s = jnp.where(lane <= query, s, -jnp.inf)
        update(s, cum_q, vnew_ref[0])
        out = acc_sc[...] / l_sc[...]
        own = _div_pow2(_iota(out.shape, 1), FOX_HEAD_DIM) == _mod_pow2(_iota(out.shape, 0), FOX_HEADS)
        out = jnp.where(own, out, 0.0)
        o_ref[0] = jnp.concatenate(
            [jnp.sum(out[i * FOX_HEADS:(i + 1) * FOX_HEADS], axis=0, keepdims=True) for i in range(n_new)], axis=0)


def _fox_sample(page_table, q_bd, k_pool_t, v_pool_t, lf_pool_t, k_new_t, v_new_t, lf_new_t, pages_per_step):
    n_seq, n_pages = page_table.shape
    n_rows = q_bd.shape[1]
    n_new = n_rows // FOX_HEADS
    page = k_pool_t.shape[2]
    pp = pages_per_step
    steps = n_pages // pp

    def paged(shape, j):
        return pl.BlockSpec(shape, lambda b, s, pt: (pt[b * n_pages + s * pp + j], 0, 0))

    per_seq = lambda shape: pl.BlockSpec(shape, lambda b, s, pt: (b, 0, 0))
    in_specs = ([per_seq((1, n_rows, FOX_WIDTH))]
                + [paged((1, FOX_WIDTH, page), j) for j in range(pp)]
                + [paged((1, FOX_WIDTH, page), j) for j in range(pp)]
                + [paged((1, FOX_HEADS, page), j) for j in range(pp)]
                + [per_seq((1, FOX_WIDTH, LANES)), per_seq((1, FOX_WIDTH, LANES)), per_seq((1, FOX_HEADS, LANES))])
    grid_spec = pltpu.PrefetchScalarGridSpec(
        num_scalar_prefetch=1,
        grid=(n_seq, steps),
        in_specs=in_specs,
        out_specs=per_seq((1, n_new, FOX_WIDTH)),
        scratch_shapes=[pltpu.VMEM((n_rows, 1), F32), pltpu.VMEM((n_rows, 1), F32),
                        pltpu.VMEM((n_rows, FOX_WIDTH), F32)],
    )
    return pl.pallas_call(
        functools.partial(_fox_sample_kernel, pages_per_step=pp, n_new=n_new),
        grid_spec=grid_spec,
        out_shape=jax.ShapeDtypeStruct((n_seq, n_new, FOX_WIDTH), F32),
        compiler_params=pltpu.CompilerParams(dimension_semantics=("arbitrary", "arbitrary"),
                                             vmem_limit_bytes=VMEM_LIMIT),
        name="fox_sample_attention",
    )(page_table.reshape(-1), q_bd, *([k_pool_t] * pp), *([v_pool_t] * pp), *([lf_pool_t] * pp),
      k_new_t, v_new_t, lf_new_t)


def _dot1(a, b):
    return _dot(a.astype(BF16), b.astype(BF16))


def _unit_lower_inverses(mats, same_block, chunk, base, dot):
    t_len = mats[0].shape[0]
    eye = (_iota((t_len, t_len), 0) == _iota((t_len, t_len), 1)).astype(F32)
    size = min(base, chunk)
    in_base = same_block(size)
    ns = [jnp.where(in_base, -a, 0.0) for a in mats]
    invs = [eye + n for n in ns]
    power = 1
    while 2 * power < size:
        ns = [dot(n, n) for n in ns]
        invs = [inv + dot(inv, n) for inv, n in zip(invs, ns)]
        power *= 2
    while size < chunk:
        off_mask = same_block(2 * size) & jnp.logical_not(same_block(size))
        corr = [dot(jnp.where(off_mask, a, 0.0), inv) for a, inv in zip(mats, invs)]
        invs = [inv - dot(inv, c) for inv, c in zip(invs, corr)]
        size *= 2
    return invs


def _short_conv(xbuf, conv_w, n_rows):
    first = SUBLANES - GDN_CONV_WIDTH + 1
    y = conv_w[0:1] * xbuf[first:first + n_rows, :]
    for i in range(1, GDN_CONV_WIDTH):
        y = y + conv_w[i:i + 1] * xbuf[first + i:first + i + n_rows, :]
    return y * _sigmoid(y)


class _ChunkMasks:
    def __init__(self, t_len, chunk):
        self.t_len, self.chunk = t_len, chunk
        ri = _iota((t_len, t_len), 0)
        ci = _iota((t_len, t_len), 1)
        self.same_block = lambda size: _div_pow2(ri, size) == _div_pow2(ci, size)
        in_chunk = self.same_block(chunk)
        self.lower = in_chunk & (ri >= ci)
        self.strict = in_chunk & (ri > ci)


def _chunk_cumsums(sblk, masks):
    chunk = masks.chunk
    cum_all = _dot_exact_lhs(masks.lower.astype(BF16), sblk)
    last_all = jnp.concatenate(
        [jnp.broadcast_to(cum_all[(c + 1) * chunk - 1:(c + 1) * chunk, :], (chunk, LANES))
         for c in range(masks.t_len // chunk)], axis=0)
    return cum_all, cum_all.T, last_all


def _delta_rule_chunk_terms(ybuf, sblks, cums, masks):
    sub = masks.t_len
    units = [(i, h) for i in range(len(sblks)) for h in range(GDN_HEADS)]
    col = lambda a, base, h: a[:, base + h:base + h + 1]
    betas = [col(sblks[i], ROW_BETA, h) for i, h in units]
    cumcols = [col(cums[i][0], ROW_G, h) for i, h in units]
    lasts = [col(cums[i][2], ROW_G, h) for i, h in units]
    decays = [jnp.where(masks.lower,
                        jnp.exp(jnp.where(masks.lower, cumcols[u] - cums[i][1][ROW_G + h:ROW_G + h + 1, :], 0.0)), 0.0)
              for u, (i, h) in enumerate(units)]
    qs, ks, vs = [], [], []
    for i, h in units:
        rows = slice(i * sub, (i + 1) * sub)
        q = ybuf[rows, h * GDN_KEY_DIM:(h + 1) * GDN_KEY_DIM]
        k = ybuf[rows, GDN_QK_WIDTH + h * GDN_KEY_DIM:GDN_QK_WIDTH + (h + 1) * GDN_KEY_DIM]
        qs.append(q * lax.rsqrt(jnp.sum(q * q, axis=-1, keepdims=True) + NORM_EPS) * (GDN_KEY_DIM ** -0.5))
        ks.append(k * lax.rsqrt(jnp.sum(k * k, axis=-1, keepdims=True) + NORM_EPS))
        vs.append(ybuf[rows, 2 * GDN_QK_WIDTH + h * GDN_VAL_DIM:2 * GDN_QK_WIDTH + (h + 1) * GDN_VAL_DIM])
    n = range(len(units))
    k_betas = [ks[u] * betas[u] for u in n]
    e_cums = [jnp.exp(cumcols[u]) for u in n]
    k_bfs = [ks[u].astype(BF16) for u in n]
    amats = [jnp.where(masks.strict, _dot_nt(k_betas[u].astype(BF16), k_bfs[u]) * decays[u], 0.0) for u in n]
    tmats = _unit_lower_inverses(amats, masks.same_block, masks.chunk, base=SUBLANES, dot=_dot1)
    uws = [_dot(tmats[u].astype(BF16),
                jnp.concatenate([vs[u] * betas[u], k_betas[u] * e_cums[u]], axis=-1).astype(BF16)) for u in n]
    qks = [(_dot_nt(qs[u].astype(BF16), k_bfs[u]) * decays[u]).astype(BF16) for u in n]
    flat = [(uws[u][:, :GDN_VAL_DIM], uws[u][:, GDN_VAL_DIM:].astype(BF16), qks[u], (qs[u] * e_cums[u]).astype(BF16),
             (ks[u] * jnp.exp(lasts[u] - cumcols[u])).astype(BF16), jnp.exp(lasts[u])) for u in n]
    return [flat[i * GDN_HEADS:(i + 1) * GDN_HEADS] for i in range(len(sblks))]


def _gated_out_norm(o, gz, gnorm):
    o = o * lax.rsqrt(jnp.mean(o * o, axis=-1, keepdims=True) + NORM_EPS) * gnorm
    return o * (gz * _sigmoid(gz))


def _gdn_prompt_kernel(gqkv_ref, s_ref, gz_ref, convw_ref, conv0_ref, s0_ref, gnorm_ref, o_ref, sfin_ref,
                       xbuf, ybuf, state, *, t_len, chunk):
    t = pl.program_id(1)

    @pl.when(t == 0)
    def _():
        xbuf[0:SUBLANES, :] = conv0_ref[0]
        state[...] = s0_ref[0]

    xbuf[SUBLANES:SUBLANES + t_len, :] = gqkv_ref[...]
    ybuf[...] = _short_conv(xbuf, convw_ref[...], t_len)
    xbuf[0:SUBLANES, :] = xbuf[t_len:t_len + SUBLANES, :]

    sub = min(t_len, LANES)
    masks = _ChunkMasks(sub, chunk)
    sblks = [s_ref[i * sub:(i + 1) * sub, :] for i in range(t_len // sub)]
    terms = _delta_rule_chunk_terms(ybuf, sblks, [_chunk_cumsums(sb, masks) for sb in sblks], masks)
    outs = [[] for _ in range(GDN_HEADS)]
    for sub_terms in terms:
        for c in range(sub // chunk):
            rows = slice(c * chunk, (c + 1) * chunk)
            for h, (u, wmat, qk, q_dec, k_dec, chunk_decay) in enumerate(sub_terms):
                st = state[h]
                ws = _dot(jnp.concatenate([wmat[rows], q_dec[rows]], axis=0), st.astype(BF16))
                v_new_bf = (u[rows] - ws[:chunk]).astype(BF16)
                outs[h].append(ws[chunk:] + _dot(qk[rows, rows], v_new_bf))
                state[h] = st * chunk_decay[c * chunk:c * chunk + 1] + _dot_tn(k_dec[rows], v_new_bf)
    for h in range(GDN_HEADS):
        head_cols = slice(h * GDN_VAL_DIM, (h + 1) * GDN_VAL_DIM)
        o_ref[:, head_cols] = _gated_out_norm(jnp.concatenate(outs[h], axis=0), gz_ref[:, head_cols],
                                              gnorm_ref[...]).astype(o_ref.dtype)

    @pl.when(t == pl.num_programs(1) - 1)
    def _():
        sfin_ref[0] = state[...]


def _gdn_prompt(gqkv, s, gz, conv_w, conv0, s0, gnorm, n_seq, t_len, chunk):
    n = gqkv.shape[0]
    tiles = n // n_seq // t_len
    rows = lambda width: pl.BlockSpec((t_len, width), lambda b, t: (b * tiles + t, 0))
    const = lambda shape: pl.BlockSpec(shape, lambda b, t: (0,) * len(shape))
    state_spec = pl.BlockSpec((1, GDN_HEADS, GDN_KEY_DIM, GDN_VAL_DIM), lambda b, t: (b, 0, 0, 0))
    return pl.pallas_call(
        functools.partial(_gdn_prompt_kernel, t_len=t_len, chunk=chunk),
        grid=(n_seq, tiles),
        in_specs=[rows(GDN_CONV_CH), rows(LANES), rows(GDN_V_WIDTH), const((GDN_CONV_WIDTH, GDN_CONV_CH)),
                  pl.BlockSpec((1, SUBLANES, GDN_CONV_CH), lambda b, t: (b, 0, 0)), state_spec,
                  const((1, GDN_VAL_DIM))],
        out_specs=(rows(GDN_V_WIDTH), state_spec),
        out_shape=(jax.ShapeDtypeStruct((n, GDN_V_WIDTH), BF16),
                   jax.ShapeDtypeStruct((n_seq, GDN_HEADS, GDN_KEY_DIM, GDN_VAL_DIM), F32)),
        scratch_shapes=[pltpu.VMEM((t_len + SUBLANES, GDN_CONV_CH), F32), pltpu.VMEM((t_len, GDN_CONV_CH), F32),
                        pltpu.VMEM((GDN_HEADS, GDN_KEY_DIM, GDN_VAL_DIM), F32)],
        compiler_params=pltpu.CompilerParams(dimension_semantics=("arbitrary", "arbitrary"),
                                             vmem_limit_bytes=VMEM_LIMIT),
        name="gated_deltanet_prompt",
    )(gqkv, s, gz, conv_w, conv0, s0, gnorm)


def _gdn_sample_kernel(xin_ref, s_ref, gz_ref, convw_ref, s0_ref, gnorm_ref, o_ref, sfin_ref, xbuf, ybuf,
                       *, n_seq, chunk):
    per_seq = SUBLANES + chunk
    n_in = n_seq * per_seq
    t_len = n_seq * chunk
    xbuf[0:SUBLANES, :] = jnp.zeros((SUBLANES, GDN_CONV_CH), F32)
    xbuf[SUBLANES:SUBLANES + n_in, :] = xin_ref[...]
    y = _short_conv(xbuf, convw_ref[...], n_in)
    for i in range(n_seq):
        ybuf[i * chunk:(i + 1) * chunk, :] = y[i * per_seq + SUBLANES:(i + 1) * per_seq]

    masks = _ChunkMasks(t_len, chunk)
    sblk = s_ref[...]
    cums = _chunk_cumsums(sblk, masks)
    seq_of_row = _div_pow2(_iota((t_len, GDN_KEY_DIM), 0), chunk)
    terms = _delta_rule_chunk_terms(ybuf, [sblk], [cums], masks)[0]
    for h, (u, wmat, qk, q_dec, k_dec, chunk_decay) in enumerate(terms):
        v_news, reads = [], []
        for i in range(n_seq):
            rows = slice(i * chunk, (i + 1) * chunk)
            ws = _dot(jnp.concatenate([wmat[rows], q_dec[rows]], axis=0), s0_ref[i, h].astype(BF16))
            v_news.append(u[rows] - ws[:chunk])
            reads.append(ws[chunk:])
        v_new_bf = jnp.concatenate(v_news, axis=0).astype(BF16)
        o = jnp.concatenate(reads, axis=0) + _dot(qk, v_new_bf)
        for i in range(n_seq):
            own_rows = jnp.where(seq_of_row == i, k_dec, jnp.zeros_like(k_dec))
            sfin_ref[i, h] = (s0_ref[i, h] * chunk_decay[i * chunk:i * chunk + 1]
                              + _dot_tn(own_rows, v_new_bf))
        head_cols = slice(h * GDN_VAL_DIM, (h + 1) * GDN_VAL_DIM)
        o_ref[:, head_cols] = _gated_out_norm(o, gz_ref[:, head_cols], gnorm_ref[...])


def _gdn_sample(xin, s, gz, conv_w, s0, gnorm, n_seq_total, seq_per_step, chunk):
    steps = n_seq_total // seq_per_step
    per_seq = SUBLANES + chunk
    t_len = seq_per_step * chunk
    rows = lambda width: pl.BlockSpec((t_len, width), lambda i: (i, 0))
    const = lambda shape: pl.BlockSpec(shape, lambda i: (0,) * len(shape))
    state_spec = pl.BlockSpec((seq_per_step, GDN_HEADS, GDN_KEY_DIM, GDN_VAL_DIM), lambda i: (i, 0, 0, 0))
    return pl.pallas_call(
        functools.partial(_gdn_sample_kernel, n_seq=seq_per_step, chunk=chunk),
        grid=(steps,),
        in_specs=[pl.BlockSpec((seq_per_step * per_seq, GDN_CONV_CH), lambda i: (i, 0)), rows(LANES),
                  rows(GDN_V_WIDTH), const((GDN_CONV_WIDTH, GDN_CONV_CH)), state_spec, const((1, GDN_VAL_DIM))],
        out_specs=(rows(GDN_V_WIDTH), state_spec),
        out_shape=(jax.ShapeDtypeStruct((n_seq_total * chunk, GDN_V_WIDTH), F32),
                   jax.ShapeDtypeStruct((n_seq_total, GDN_HEADS, GDN_KEY_DIM, GDN_VAL_DIM), F32)),
        scratch_shapes=[pltpu.VMEM((seq_per_step * per_seq + SUBLANES, GDN_CONV_CH), F32),
                        pltpu.VMEM((t_len, GDN_CONV_CH), F32)],
        compiler_params=pltpu.CompilerParams(dimension_semantics=("arbitrary",), vmem_limit_bytes=VMEM_LIMIT),
        name="gated_deltanet_sample",
    )(xin, s, gz, conv_w, s0, gnorm)


def _post_kernel(x_ref, oa_ref, ob_ref, gates_ref, ple_ref, wa_ref, wb_ref, wout_ref, gmlp_ref, wup_ref,
                 wdown_ref, gple_ref, wpg_ref, wple_ref, y_ref):
    def rms(a, g):
        return a * lax.rsqrt(jnp.mean(a * a, axis=-1, keepdims=True) + NORM_EPS) * g

    tm = x_ref.shape[0]
    n_sub = 1
    subs = [slice(i * (tm // n_sub), (i + 1) * (tm // n_sub)) for i in range(n_sub)]
    gates = [_sigmoid(gates_ref[r, :].astype(F32)) for r in subs]
    a = [_dot(oa_ref[r, :].astype(BF16), wa_ref[...]) for r in subs]
    b = [_dot(ob_ref[r, :].astype(BF16), wb_ref[...]) for r in subs]
    merged = [(g[:, :D_MODEL] * ai + g[:, D_MODEL:] * bi).astype(BF16) for g, ai, bi in zip(gates, a, b)]
    x = [x_ref[r, :] + _dot(m, wout_ref[...]) for r, m in zip(subs, merged)]
    h = [rms(xi, gmlp_ref[...]).astype(BF16) for xi in x]
    up = [jnp.maximum(_dot(hi, wup_ref[...]), 0.0) for hi in h]
    x = [xi + _dot((u * u).astype(BF16), wdown_ref[...]) for xi, u in zip(x, up)]
    hp = [rms(xi, gple_ref[...]).astype(BF16) for xi in x]
    ple_gate = [_sigmoid(_dot(hi, wpg_ref[...])) for hi in hp]
    for r, xi, g in zip(subs, x, ple_gate):
        y_ref[r, :] = xi + g * _dot(ple_ref[r, :].astype(BF16), wple_ref[...])


def _post(x, oa, ob, gates, ple, w, tm):
    n = x.shape[0]
    rows = lambda width: pl.BlockSpec((tm, width), lambda i: (i, 0))
    resident = lambda shape: pl.BlockSpec(shape, lambda i: (0, 0), pipeline_mode=pl.Buffered(1))
    return pl.pallas_call(
        _post_kernel,
        grid=(n // tm,),
        in_specs=[rows(D_MODEL), rows(FOX_WIDTH), rows(GDN_V_WIDTH), rows(2 * D_MODEL), rows(PLE_DIM),
                  resident((FOX_WIDTH, D_MODEL)), resident((GDN_V_WIDTH, D_MODEL)), resident((D_MODEL, D_MODEL)),
                  resident((1, D_MODEL)), resident((D_MODEL, D_FF)), resident((D_FF, D_MODEL)),
                  resident((1, D_MODEL)), resident((D_MODEL, D_MODEL)), resident((PLE_DIM, D_MODEL))],
        out_specs=rows(D_MODEL),
        out_shape=jax.ShapeDtypeStruct((n, D_MODEL), F32),
        compiler_params=pltpu.CompilerParams(dimension_semantics=("arbitrary",), vmem_limit_bytes=VMEM_LIMIT),
        name="merge_mlp_ple",
    )(x, oa, ob, gates, ple, w["w_a"], w["w_b"], w["w_out"], w["gmlp"], w["w_up"], w["w_down"], w["gple"],
      w["w_pg"], w["w_ple"])


def _prepare_weights(l, norm_mix_g, w_in, fox_f_bias, fox_q_norm_g, fox_k_norm_g, gdn_conv_w, gdn_a_log, gdn_dt_bias,
                     gdn_out_norm_g, w_branch_a, w_branch_b, w_out, norm_mlp_g, w_up, w_down, norm_ple_g,
                     w_ple_gate, w_ple):
    wi = w_in[l]
    o_ff = 3 * FOX_WIDTH
    o_gqkv = o_ff + FOX_HEADS
    o_ga = o_gqkv + GDN_CONV_CH
    o_gb = o_ga + GDN_HEADS
    o_gz = o_gb + GDN_HEADS
    o_gates = o_gz + GDN_V_WIDTH
    w_main = jnp.concatenate([wi[:, :o_ff], wi[:, o_gqkv:o_ga], wi[:, o_gz:]], axis=1).astype(BF16)
    w_small = jnp.concatenate([wi[:, o_ff:o_gqkv], wi[:, o_ga:o_gz]], axis=1)
    w_small = jnp.pad(w_small, ((0, 0), (0, LANES - N_SMALL))).astype(BF16)
    pad_col = lambda parts: jnp.pad(jnp.concatenate(parts), (0, LANES - N_SMALL)).reshape(LANES, 1).astype(F32)
    zeros_h = jnp.zeros((GDN_HEADS,), F32)
    head = jnp.arange(FOX_WIDTH) // FOX_HEAD_DIM
    return dict(
        gmix=norm_mix_g[l].reshape(1, D_MODEL),
        w_main=w_main,
        w_small=w_small,
        gmat=((head[:, None] == head[None, :]).astype(F32) / FOX_HEAD_DIM).astype(BF16),
        qg=jnp.tile(fox_q_norm_g[l], FOX_HEADS).reshape(1, FOX_WIDTH),
        kg=jnp.tile(fox_k_norm_g[l], FOX_HEADS).reshape(1, FOX_WIDTH),
        sbias=pad_col([fox_f_bias[l], gdn_dt_bias[l], zeros_h]),
        alog=pad_col([jnp.zeros((FOX_HEADS,), F32), gdn_a_log[l], zeros_h]),
        conv_w=gdn_conv_w[l],
        gnorm=gdn_out_norm_g[l].reshape(1, GDN_VAL_DIM),
        w_a=w_branch_a[l].astype(BF16),
        w_b=w_branch_b[l].astype(BF16),
        w_out=w_out[l].astype(BF16),
        gmlp=norm_mlp_g[l].reshape(1, D_MODEL),
        w_up=w_up[l].astype(BF16),
        w_down=w_down[l].astype(BF16),
        gple=norm_ple_g[l].reshape(1, D_MODEL),
        w_pg=w_ple_gate[l].astype(BF16),
        w_ple=w_ple[l].astype(BF16),
    )


def _pick_tile(n, target):
    t = min(n, target)
    while n % t:
        t //= 2
    return t


def _prompt_layer(x, ple, w):
    b, seq_len, _ = x.shape
    n = b * seq_len
    tm = _pick_tile(n, 512)
    q_bf, k_t, k_bf, v_t, v_bf, gqkv, gz, gates, s, st = _in_proj(
        x.reshape(n, D_MODEL), w, tm, FOX_HEAD_DIM ** -0.5 * LOG2E, kv_seq_len=seq_len)

    cum = _seq_cumsum(st, seq_len, LOG2E)
    ck = cum.reshape(FOX_HEADS, b, 1, seq_len).transpose(1, 0, 2, 3)
    tq = _pick_tile(seq_len, 512)
    o_a = _fox_prompt(q_bf.reshape(b, seq_len, FOX_WIDTH), k_bf.reshape(b, seq_len, FOX_WIDTH),
                      v_bf.reshape(b, seq_len, FOX_WIDTH), ck, tq)

    chunk = math.gcd(seq_len, GDN_CHUNK)
    t_len = _pick_tile(seq_len, 8 * chunk)
    conv0 = jnp.zeros((b, SUBLANES, GDN_CONV_CH), F32)
    ssm0 = jnp.zeros((b, GDN_HEADS, GDN_KEY_DIM, GDN_VAL_DIM), F32)
    o_b, ssm = _gdn_prompt(gqkv, s, gz, w["conv_w"], conv0, ssm0, w["gnorm"], b, t_len, chunk)

    y = _post(x.reshape(n, D_MODEL), o_a.reshape(n, FOX_WIDTH), o_b, gates, ple.reshape(n, PLE_DIM), w,
              _pick_tile(n, 512))
    keep = GDN_CONV_WIDTH - 1
    token_major = lambda a: a.reshape(b, FOX_HEADS, FOX_HEAD_DIM, seq_len).transpose(0, 3, 1, 2)
    states = (token_major(k_t), token_major(v_t),
              st[ROW_LOGF:ROW_LOGF + FOX_HEADS].reshape(FOX_HEADS, b, seq_len).transpose(1, 2, 0),
              gqkv.reshape(b, seq_len, GDN_CONV_CH)[:, seq_len - keep:], ssm)
    return y.reshape(b, seq_len, D_MODEL), states


def _sample_layer(x, ple, w, k_pool, v_pool, lf_pool, conv_buf, ssm_state, page_table):
    b, s_new, _ = x.shape
    n = b * s_new
    keep = GDN_CONV_WIDTH - 1
    assert s_new >= keep and s_new <= SUBLANES
    tm = _pick_tile(n, 256)
    q_bf, k, k_bf, v, v_bf, gqkv, gz, gates, s, st = _in_proj(x.reshape(n, D_MODEL), w, tm, FOX_HEAD_DIM ** -0.5)

    n_pool, page = k_pool.shape[:2]
    q4 = q_bf.reshape(b, s_new, FOX_HEADS, 1, FOX_HEAD_DIM)
    eye = jnp.eye(FOX_HEADS, dtype=BF16).reshape(1, 1, FOX_HEADS, FOX_HEADS, 1)
    q_bd = (q4 * eye).reshape(b, s_new * FOX_HEADS, FOX_WIDTH)
    pad_keys = lambda a: jnp.pad(a.reshape(b, s_new, FOX_WIDTH).transpose(0, 2, 1),
                                 ((0, 0), (0, 0), (0, LANES - s_new)))
    lf_new_t = jnp.pad(st[ROW_LOGF:ROW_LOGF + FOX_HEADS].reshape(FOX_HEADS, b, s_new).transpose(1, 0, 2),
                       ((0, 0), (0, 0), (0, LANES - s_new)))
    pool_t = lambda a: a.transpose(0, 2, 3, 1).reshape(n_pool, FOX_WIDTH, page)
    o_a = _fox_sample(page_table, q_bd, pool_t(k_pool), pool_t(v_pool), lf_pool.transpose(0, 2, 1),
                      pad_keys(k_bf), pad_keys(v_bf), lf_new_t, pages_per_step=_pick_tile(page_table.shape[1], 32))

    chunk = SUBLANES
    pad_tok = lambda a: jnp.pad(a.reshape(b, s_new, -1), ((0, 0), (0, chunk - s_new), (0, 0)))
    xin = jnp.concatenate([jnp.pad(conv_buf, ((0, 0), (SUBLANES - keep, 0), (0, 0))), pad_tok(gqkv)], axis=1)
    seq_per_step = _pick_tile(b, LANES // chunk)
    o_b, ssm = _gdn_sample(xin.reshape(b * (SUBLANES + chunk), GDN_CONV_CH), pad_tok(s).reshape(b * chunk, LANES),
                           pad_tok(gz).reshape(b * chunk, GDN_V_WIDTH), w["conv_w"], ssm_state, w["gnorm"],
                           b, seq_per_step, chunk)
    o_b = o_b.reshape(b, chunk, GDN_V_WIDTH)[:, :s_new].reshape(n, GDN_V_WIDTH)

    y = _post(x.reshape(n, D_MODEL), o_a.reshape(n, FOX_WIDTH), o_b, gates, ple.reshape(n, PLE_DIM), w, tm)
    states = (k.reshape(b, s_new, FOX_HEADS, FOX_HEAD_DIM), v.reshape(b, s_new, FOX_HEADS, FOX_HEAD_DIM),
              s[:, ROW_LOGF:ROW_LOGF + FOX_HEADS].reshape(b, s_new, FOX_HEADS),
              gqkv.reshape(b, s_new, GDN_CONV_CH)[:, s_new - keep:], ssm)
    return y.reshape(b, s_new, D_MODEL), states


def kernel(x_prompt, x_sample, p_prompt, p_sample, cache_k, cache_v, cache_logf, state_conv, state_ssm, page_table,
           norm_mix_g, w_in, fox_f_bias, fox_q_norm_g, fox_k_norm_g, gdn_conv_w, gdn_a_log, gdn_dt_bias,
           gdn_out_norm_g, w_branch_a, w_branch_b, w_out, norm_mlp_g, w_up, w_down, norm_ple_g, w_ple_gate, w_ple):
    depth = w_in.shape[0]
    y_prompt, y_sample = x_prompt, x_sample
    prompt_states, sample_states = [], []
    for l in range(depth):
        w = _prepare_weights(l, norm_mix_g, w_in, fox_f_bias, fox_q_norm_g, fox_k_norm_g, gdn_conv_w, gdn_a_log,
                             gdn_dt_bias, gdn_out_norm_g, w_branch_a, w_branch_b, w_out, norm_mlp_g, w_up, w_down,
                             norm_ple_g, w_ple_gate, w_ple)
        y_prompt, st_p = _prompt_layer(y_prompt, p_prompt[l], w)
        y_sample, st_s = _sample_layer(y_sample, p_sample[l], w, cache_k[l], cache_v[l], cache_logf[l],
                                       state_conv[l], state_ssm[l], page_table)
        prompt_states.append(st_p)
        sample_states.append(st_s)
    stack = lambda states, i: jnp.stack([st[i] for st in states])
    return ((y_prompt, y_sample) + tuple(stack(prompt_states, i) for i in range(5))
            + tuple(stack(sample_states, i) for i in range(5)))
```

```python
import functools
import math

import jax
import jax.numpy as jnp
from jax import lax
from jax.experimental import pallas as pl
from jax.experimental.pallas import tpu as pltpu

F32 = jnp.float32
BF16 = jnp.bfloat16

D_MODEL = 1024
FOX_HEADS = 8
FOX_HEAD_DIM = 64
FOX_WIDTH = FOX_HEADS * FOX_HEAD_DIM
GDN_HEADS = 4
GDN_KEY_DIM = 128
GDN_VAL_DIM = 128
GDN_QK_WIDTH = GDN_HEADS * GDN_KEY_DIM
GDN_V_WIDTH = GDN_HEADS * GDN_VAL_DIM
GDN_CONV_WIDTH = 4
GDN_CONV_CH = 2 * GDN_QK_WIDTH + GDN_V_WIDTH
GDN_CHUNK = 64
D_FF = 4 * D_MODEL
PLE_DIM = 256
NORM_EPS = 1e-6

LANES = 128
SUBLANES = 8
NEG_BIG = -1e30
LOG2E = 1.4426950408889634
ZERO_PROB_LOG2 = 152.0
NORM_SLACK = 1.0 + 2.0 ** -7
VMEM_LIMIT = 56 * 1024 * 1024

_MAIN_SEGS = (FOX_WIDTH, FOX_WIDTH, FOX_WIDTH, GDN_CONV_CH, GDN_V_WIDTH, 2 * D_MODEL)
_MAIN_OFFS = tuple(sum(_MAIN_SEGS[:i]) for i in range(len(_MAIN_SEGS) + 1))
D_MAIN = _MAIN_OFFS[-1]
ROW_LOGF = 0
ROW_G = FOX_HEADS
ROW_BETA = FOX_HEADS + GDN_HEADS
N_SMALL = FOX_HEADS + 2 * GDN_HEADS


def _dot(a, b):
    return jnp.dot(a, b, preferred_element_type=F32)


def _dot_nt(a, b):
    return lax.dot_general(a, b, (((1,), (1,)), ((), ())), preferred_element_type=F32)


def _dot_tn(a, b):
    return lax.dot_general(a, b, (((0,), (0,)), ((), ())), preferred_element_type=F32)


def _split(a, parts, axis):
    pieces = []
    for _ in range(parts - 1):
        p = a.astype(BF16).astype(F32)
        pieces.append(p)
        a = a - p
    pieces.append(a)
    return jnp.concatenate(pieces, axis=axis).astype(BF16)


def _dot3(a, b):
    m, n = a.shape[0], b.shape[1]
    r = _dot(_split(a, 2, 0), _split(b, 2, 1))
    return (r[:m, :n] + r[:m, n:]) + (r[m:, :n] + r[m:, n:])


def _dot_exact_lhs(a_bf, b):
    n = b.shape[1]
    r = _dot(a_bf, _split(b, 3, 1))
    return r[:, :n] + (r[:, n:2 * n] + r[:, 2 * n:])


def _dot_exact_rhs(a, b_bf):
    m = a.shape[0]
    r = _dot(_split(a, 3, 0), b_bf)
    return r[:m] + (r[m:2 * m] + r[2 * m:])


def _sigmoid(x):
    return 1.0 / (1.0 + jnp.exp(-x))


def _iota(shape, dim):
    return lax.broadcasted_iota(jnp.int32, shape, dim)


def _div_pow2(x, divisor):
    shift = divisor.bit_length() - 1
    assert divisor == 1 << shift
    return lax.shift_right_logical(x, shift)


def _mod_pow2(x, divisor):
    assert divisor & (divisor - 1) == 0
    return x & (divisor - 1)


def _in_proj_kernel(x_ref, gmix_ref, wmain_ref, wsmall_ref, gmat_ref, qg_ref, kg_ref, sbias_ref, alog_ref,
                    q_ref, k_ref, kb_ref, v_ref, vb_ref, gqkv_ref, gz_ref, gates_ref, s_ref, st_ref,
                    *, feature_major_kv, q_scale):
    x = x_ref[...]
    xn = x * lax.rsqrt(jnp.mean(x * x, axis=-1, keepdims=True) + NORM_EPS) * gmix_ref[...]
    xb = xn.astype(BF16)

    def proj(seg):
        return _dot(xb, wmain_ref[:, _MAIN_OFFS[seg]:_MAIN_OFFS[seg + 1]])

    gmat = gmat_ref[...]

    def head_norm(t, g):
        ms = _dot((t * t).astype(BF16), gmat)
        return t * lax.rsqrt(ms + NORM_EPS) * g

    q = head_norm(proj(0), qg_ref[...])
    q_ref[...] = (q * q_scale).astype(BF16)
    k = head_norm(proj(1), kg_ref[...])
    kb_ref[...] = k.astype(BF16)
    v = proj(2)
    vb_ref[...] = v.astype(BF16)
    if feature_major_kv:
        k_ref[0] = k.T
        v_ref[0] = v.T
    else:
        k_ref[...] = k
        v_ref[...] = v
    gqkv_ref[...] = proj(3)
    gz_ref[...] = proj(4)
    gates_ref[...] = proj(5).astype(gates_ref.dtype)

    z = _dot(xb, wsmall_ref[...]).T + sbias_ref[...]
    row = _iota(z.shape, 0)
    t = jnp.log1p(jnp.exp(-jnp.abs(z)))
    logf = jnp.minimum(z, 0.0) - t
    g = -jnp.exp(alog_ref[...]) * (jnp.maximum(z, 0.0) + t)
    beta = _sigmoid(z)
    res = jnp.where(row < ROW_G, logf, jnp.where(row < ROW_BETA, g, jnp.where(row < N_SMALL, beta, 0.0)))
    st_ref[...] = res[:N_SMALL]
    s_ref[...] = res.T


def _in_proj(x, w, tm, q_scale, kv_seq_len=None):
    n = x.shape[0]
    const = lambda i: (0, 0)
    rows = lambda width: pl.BlockSpec((tm, width), lambda i: (i, 0))
    resident = lambda shape: pl.BlockSpec(shape, const, pipeline_mode=pl.Buffered(1))
    if kv_seq_len is None:
        kv_shape, kv_spec = jax.ShapeDtypeStruct((n, FOX_WIDTH), F32), rows(FOX_WIDTH)
    else:
        tiles = kv_seq_len // tm
        kv_shape = jax.ShapeDtypeStruct((n // kv_seq_len, FOX_WIDTH, kv_seq_len), F32)
        kv_spec = pl.BlockSpec((1, FOX_WIDTH, tm), lambda i: (i // tiles, 0, i % tiles))
    out_shape = (
        jax.ShapeDtypeStruct((n, FOX_WIDTH), BF16),
        kv_shape,
        jax.ShapeDtypeStruct((n, FOX_WIDTH), BF16),
        kv_shape,
        jax.ShapeDtypeStruct((n, FOX_WIDTH), BF16),
        jax.ShapeDtypeStruct((n, GDN_CONV_CH), F32),
        jax.ShapeDtypeStruct((n, GDN_V_WIDTH), F32),
        jax.ShapeDtypeStruct((n, 2 * D_MODEL), BF16),
        jax.ShapeDtypeStruct((n, LANES), F32),
        jax.ShapeDtypeStruct((N_SMALL, n), F32),
    )
    out_specs = (
        rows(FOX_WIDTH), kv_spec, rows(FOX_WIDTH), kv_spec, rows(FOX_WIDTH),
        rows(GDN_CONV_CH), rows(GDN_V_WIDTH), rows(2 * D_MODEL), rows(LANES),
        pl.BlockSpec((N_SMALL, tm), lambda i: (0, i)),
    )
    in_specs = [
        rows(D_MODEL),
        resident((1, D_MODEL)),
        resident((D_MODEL, D_MAIN)),
        resident((D_MODEL, LANES)),
        resident((FOX_WIDTH, FOX_WIDTH)),
        resident((1, FOX_WIDTH)),
        resident((1, FOX_WIDTH)),
        resident((LANES, 1)),
        resident((LANES, 1)),
    ]
    return pl.pallas_call(
        functools.partial(_in_proj_kernel, feature_major_kv=kv_seq_len is not None, q_scale=q_scale),
        grid=(n // tm,),
        in_specs=in_specs,
        out_specs=out_specs,
        out_shape=out_shape,
        compiler_params=pltpu.CompilerParams(dimension_semantics=("arbitrary",), vmem_limit_bytes=VMEM_LIMIT),
        name="in_proj",
    )(x, w["gmix"], w["w_main"], w["w_small"], w["gmat"], w["qg"], w["kg"], w["sbias"], w["alog"])


def _cumsum_kernel(lf_ref, cum_ref, *, scale):
    n_chunks = lf_ref.shape[1] // LANES
    upper = (_iota((LANES, LANES), 0) <= _iota((LANES, LANES), 1)).astype(BF16)
    chunks = jnp.concatenate([lf_ref[:, c * LANES:(c + 1) * LANES] for c in range(n_chunks)], axis=0)
    local = _dot_exact_rhs(chunks, upper)
    carry = jnp.zeros((FOX_HEADS, 1), F32)
    for c in range(n_chunks):
        cs = carry + local[c * FOX_HEADS:(c + 1) * FOX_HEADS]
        cum_ref[:, c * LANES:(c + 1) * LANES] = cs * scale
        carry = cs[:, LANES - 1:LANES]


def _seq_cumsum(st, seq_len, scale):
    n = st.shape[1]
    spec = pl.BlockSpec((FOX_HEADS, seq_len), lambda b: (0, b))
    return pl.pallas_call(
        functools.partial(_cumsum_kernel, scale=scale),
        grid=(n // seq_len,),
        in_specs=[spec],
        out_specs=spec,
        out_shape=jax.ShapeDtypeStruct((FOX_HEADS, n), F32),
        compiler_params=pltpu.CompilerParams(dimension_semantics=("arbitrary",)),
        name="fox_cumsum",
    )(st)


def _fox_prompt_kernel(q_ref, k_ref, v_ref, ck_ref, o_ref, kmax_sc, ckmin_sc, *, tq):
    qi = pl.program_id(2)
    lane = _iota((tq, LANES), 1)
    first = lane < FOX_HEAD_DIM
    in_head = (first, jnp.logical_not(first))
    tile_lane = _iota((1, LANES), 1)
    heads = (0, 1)

    @pl.when(qi == 0)
    def _():
        def scan(c, carry):
            sq_max, ck_min = carry
            sl = pl.ds(pl.multiple_of(c * tq, tq), tq)
            kk = k_ref[0, sl, :].astype(F32)
            sq = kk * kk
            sq_max = tuple(jnp.maximum(sq_max[h], jnp.max(jnp.sum(jnp.where(in_head[h], sq, 0.0), axis=-1,
                                                                  keepdims=True), axis=0, keepdims=True))
                           for h in heads)
            ck_min = tuple(jnp.where(tile_lane == c, jnp.min(ck_ref[0, h, :, sl], axis=-1, keepdims=True), ck_min[h])
                           for h in heads)
            return sq_max, ck_min
        zero, zero_row = jnp.zeros((1, 1), F32), jnp.zeros((1, LANES), F32)
        sq_max, ck_min = lax.fori_loop(0, k_ref.shape[1] // tq, scan, ((zero, zero), (zero_row, zero_row)))
        for h in heads:
            kmax_sc[h] = jnp.broadcast_to(jnp.sqrt(sq_max[h]), kmax_sc.shape[1:])
            ckmin_sc[h] = jnp.broadcast_to(ck_min[h], ckmin_sc.shape[1:])

    q2 = q_ref[0]
    causal = _iota((tq, tq), 0) >= _iota((tq, tq), 1)
    qs, cqs, reach = [], [], []
    for h in heads:
        qh = jnp.where(in_head[h], q2, jnp.zeros_like(q2))
        cq_row = ck_ref[0, h, :, pl.ds(pl.multiple_of(qi * tq, tq), tq)]
        cq = jnp.broadcast_to(cq_row, (LANES, tq)).T[:, 0:1]
        qf = qh.astype(F32)
        q_norm = jnp.sqrt(jnp.sum(qf * qf, axis=-1, keepdims=True))
        qs.append(qh)
        cqs.append(cq)
        reach.append(q_norm * kmax_sc[h][0:1, 0:1] * NORM_SLACK + cq)

    def steps(hs, j, carries, diagonal):
        sl = pl.ds(pl.multiple_of(j * tq, tq), tq)
        k2 = k_ref[0, sl, :]
        v2 = v_ref[0, sl, :]
        ss = [(cqs[h] - ck_ref[0, h, :, sl]) + _dot_nt(qs[h], k2) for h in hs]
        if diagonal:
            ss = [jnp.where(causal, s, -jnp.inf) for s in ss]
        m_news = [jnp.maximum(c[0], jnp.max(s, axis=-1, keepdims=True)) for c, s in zip(carries, ss)]
        alphas = [jnp.exp2(c[0] - m_new) for c, m_new in zip(carries, m_news)]
        ps = [jnp.exp2(s - m_new) for s, m_new in zip(ss, m_news)]
        ls = [alpha * c[1] + jnp.sum(p, axis=-1, keepdims=True) for alpha, c, p in zip(alphas, carries, ps)]
        accs = [alpha * c[2] + _dot(p.astype(BF16), v2) for alpha, c, p in zip(alphas, carries, ps)]
        return tuple(zip(m_news, ls, accs))

    def tiles_reached(h, m):
        gap = jnp.max(reach[h] - m, axis=0, keepdims=True) - ckmin_sc[h][0:1, :]
        hit = jnp.logical_and(tile_lane < qi, gap >= -ZERO_PROB_LOG2)
        return jnp.sum(hit.astype(F32)).astype(jnp.int32)

    col = lambda val: jnp.full((tq, 1), val, F32)
    init = (col(NEG_BIG), col(0.0), jnp.zeros((tq, LANES), F32))
    ca, cb = steps(heads, qi, (init, init), diagonal=True)
    n_a, n_b = tiles_reached(0, ca[0]), tiles_reached(1, cb[0])
    n_both = jnp.minimum(n_a, n_b)
    ca, cb = lax.fori_loop(0, n_both, lambda i, c: steps(heads, qi - 1 - i, c, diagonal=False), (ca, cb))
    ca = lax.fori_loop(n_both, n_a, lambda i, c: steps((0,), qi - 1 - i, (c,), diagonal=False)[0], ca)
    cb = lax.fori_loop(n_both, n_b, lambda i, c: steps((1,), qi - 1 - i, (c,), diagonal=False)[0], cb)
    o_ref[0] = jnp.where(first, ca[2] * (1.0 / ca[1]), cb[2] * (1.0 / cb[1])).astype(o_ref.dtype)


def _fox_prompt(qb, kb, vb, ck, tq):
    b, seq_len, _ = qb.shape
    pairs = FOX_HEADS // 2
    return pl.pallas_call(
        functools.partial(_fox_prompt_kernel, tq=tq),
        grid=(b, pairs, seq_len // tq),
        in_specs=[
            pl.BlockSpec((1, tq, LANES), lambda b_, p, i: (b_, i, p)),
            pl.BlockSpec((1, seq_len, LANES), lambda b_, p, i: (b_, 0, p)),
            pl.BlockSpec((1, seq_len, LANES), lambda b_, p, i: (b_, 0, p)),
            pl.BlockSpec((1, 2, 1, seq_len), lambda b_, p, i: (b_, p, 0, 0)),
        ],
        out_specs=pl.BlockSpec((1, tq, LANES), lambda b_, p, i: (b_, i, p)),
        out_shape=jax.ShapeDtypeStruct((b, seq_len, FOX_WIDTH), BF16),
        scratch_shapes=[pltpu.VMEM((2, SUBLANES, LANES), F32), pltpu.VMEM((2, SUBLANES, LANES), F32)],
        compiler_params=pltpu.CompilerParams(dimension_semantics=("arbitrary", "arbitrary", "arbitrary"),
                                             vmem_limit_bytes=VMEM_LIMIT),
        name="fox_prompt_attention",
    )(qb, kb, vb, ck)


def _fox_sample_kernel(pt_ref, q_ref, *refs, pages_per_step, n_new):
    del pt_ref
    pp = pages_per_step
    k_refs, v_refs, lf_refs = refs[:pp], refs[pp:2 * pp], refs[2 * pp:3 * pp]
    knew_ref, vnew_ref, lfnew_ref, o_ref, m_sc, l_sc, acc_sc = refs[3 * pp:]
    step_id = pl.program_id(1)
    n_rows = n_new * FOX_HEADS

    @pl.when(step_id == 0)
    def _():
        m_sc[...] = jnp.full(m_sc.shape, NEG_BIG, F32)
        l_sc[...] = jnp.zeros(l_sc.shape, F32)
        acc_sc[...] = jnp.zeros(acc_sc.shape, F32)

    q = q_ref[0]
    ri = _iota((LANES, LANES), 0)
    ci = _iota((LANES, LANES), 1)
    after = (ri > ci).astype(BF16)
    upto = (ri <= ci).astype(BF16)
    tile_heads = lambda a: jnp.concatenate([a] * n_new, axis=0)

    def update(s, shift, v_t):
        m_prev = m_sc[...] + shift
        m_new = jnp.maximum(m_prev, jnp.max(s, axis=-1, keepdims=True))
        alpha = jnp.exp(m_prev - m_new)
        p = jnp.exp(s - m_new)
        l_sc[...] = alpha * l_sc[...] + jnp.sum(p, axis=-1, keepdims=True)
        acc_sc[...] = alpha * acc_sc[...] + _dot_nt(p.astype(BF16), v_t)
        m_sc[...] = m_new

    lf = jnp.concatenate([r[0] for r in lf_refs], axis=0)
    suffix = _dot_exact_rhs(lf, after)
    total = suffix[:, 0:1] + lf[:, 0:1]
    bias, later = [], jnp.zeros((FOX_HEADS, 1), F32)
    for j in reversed(range(pp)):
        rows = slice(j * FOX_HEADS, (j + 1) * FOX_HEADS)
        bias.append(tile_heads(suffix[rows] + later))
        later = later + total[rows]
    bias = jnp.concatenate(bias[::-1], axis=1)
    k_t = jnp.concatenate([r[0].astype(BF16) for r in k_refs], axis=1)
    v_t = jnp.concatenate([r[0].astype(BF16) for r in v_refs], axis=1)
    update(bias + _dot(q, k_t), tile_heads(later), v_t)

    @pl.when(step_id == pl.num_programs(1) - 1)
    def _():
        cum_rows = tile_heads(_dot_exact_rhs(lfnew_ref[0], upto))
        lane = _iota((n_rows, LANES), 1)
        query = _div_pow2(_iota((n_rows, LANES), 0), FOX_HEADS)
        cum_q = jnp.sum(jnp.where(lane == query, cum_rows, 0.0), axis=-1, keepdims=True)
        s = (cum_q - cum_rows) + _dot(q, knew_ref[0])
        s = jnp.where(lane <= query, s, -jnp.inf)
        update(s, cum_q, vnew_ref[0])
        out = acc_sc[...] / l_sc[...]
        own = _div_pow2(_iota(out.shape, 1), FOX_HEAD_DIM) == _mod_pow2(_iota(out.shape, 0), FOX_HEADS)
        out = jnp.where(own, out, 0.0)
        o_ref[0] = jnp.concatenate(
            [jnp.sum(out[i * FOX_HEADS:(i + 1) * FOX_HEADS], axis=0, keepdims=True) for i in range(n_new)], axis=0)


def _fox_sample(page_table, q_bd, k_pool_t, v_pool_t, lf_pool_t, k_new_t, v_new_t, lf_new_t, pages_per_step):
    n_seq, n_pages = page_table.shape
    n_rows = q_bd.shape[1]
    n_new = n_rows // FOX_HEADS
    page = k_pool_t.shape[2]
    pp = pages_per_step
    steps = n_pages // pp

    def paged(shape, j):
        return pl.BlockSpec(shape, lambda b, s, pt: (pt[b * n_pages + s * pp + j], 0, 0))

    per_seq = lambda shape: pl.BlockSpec(shape, lambda b, s, pt: (b, 0, 0))
    in_specs = ([per_seq((1, n_rows, FOX_WIDTH))]
                + [paged((1, FOX_WIDTH, page), j) for j in range(pp)]
                + [paged((1, FOX_WIDTH, page), j) for j in range(pp)]
                + [paged((1, FOX_HEADS, page), j) for j in range(pp)]
                + [per_seq((1, FOX_WIDTH, LANES)), per_seq((1, FOX_WIDTH, LANES)), per_seq((1, FOX_HEADS, LANES))])
    grid_spec = pltpu.PrefetchScalarGridSpec(
        num_scalar_prefetch=1,
        grid=(n_seq, steps),
        in_specs=in_specs,
        out_specs=per_seq((1, n_new, FOX_WIDTH)),
        scratch_shapes=[pltpu.VMEM((n_rows, 1), F32), pltpu.VMEM((n_rows, 1), F32),
                        pltpu.VMEM((n_rows, FOX_WIDTH), F32)],
    )
    return pl.pallas_call(
        functools.partial(_fox_sample_kernel, pages_per_step=pp, n_new=n_new),
        grid_spec=grid_spec,
        out_shape=jax.ShapeDtypeStruct((n_seq, n_new, FOX_WIDTH), F32),
        compiler_params=pltpu.CompilerParams(dimension_semantics=("arbitrary", "arbitrary"),
                                             vmem_limit_bytes=VMEM_LIMIT),
        name="fox_sample_attention",
    )(page_table.reshape(-1), q_bd, *([k_pool_t] * pp), *([v_pool_t] * pp), *([lf_pool_t] * pp),
      k_new_t, v_new_t, lf_new_t)


def _dot1(a, b):
    return _dot(a.astype(BF16), b.astype(BF16))


def _unit_lower_inverses(mats, same_block, chunk, base, dot):
    t_len = mats[0].shape[0]
    eye = (_iota((t_len, t_len), 0) == _iota((t_len, t_len), 1)).astype(F32)
    size = min(base, chunk)
    in_base = same_block(size)
    ns = [jnp.where(in_base, -a, 0.0) for a in mats]
    invs = [eye + n for n in ns]
    power = 1
    while 2 * power < size:
        ns = [dot(n, n) for n in ns]
        invs = [inv + dot(inv, n) for inv, n in zip(invs, ns)]
        power *= 2
    while size < chunk:
        off_mask = same_block(2 * size) & jnp.logical_not(same_block(size))
        corr = [dot(jnp.where(off_mask, a, 0.0), inv) for a, inv in zip(mats, invs)]
        invs = [inv - dot(inv, c) for inv, c in zip(invs, corr)]
        size *= 2
    return invs


def _short_conv(xbuf, conv_w, n_rows):
    first = SUBLANES - GDN_CONV_WIDTH + 1
    y = conv_w[0:1] * xbuf[first:first + n_rows, :]
    for i in range(1, GDN_CONV_WIDTH):
        y = y + conv_w[i:i + 1] * xbuf[first + i:first + i + n_rows, :]
    return y * _sigmoid(y)


class _ChunkMasks:
    def __init__(self, t_len, chunk):
        self.t_len, self.chunk = t_len, chunk
        ri = _iota((t_len, t_len), 0)
        ci = _iota((t_len, t_len), 1)
        self.same_block = lambda size: _div_pow2(ri, size) == _div_pow2(ci, size)
        in_chunk = self.same_block(chunk)
        self.lower = in_chunk & (ri >= ci)
        self.strict = in_chunk & (ri > ci)


def _chunk_cumsums(sblk, masks):
    chunk = masks.chunk
    cum_all = _dot_exact_lhs(masks.lower.astype(BF16), sblk)
    last_all = jnp.concatenate(
        [jnp.broadcast_to(cum_all[(c + 1) * chunk - 1:(c + 1) * chunk, :], (chunk, LANES))
         for c in range(masks.t_len // chunk)], axis=0)
    return cum_all, cum_all.T, last_all


def _delta_rule_chunk_terms(ybuf, sblks, cums, masks):
    sub = masks.t_len
    units = [(i, h) for i in range(len(sblks)) for h in range(GDN_HEADS)]
    col = lambda a, base, h: a[:, base + h:base + h + 1]
    betas = [col(sblks[i], ROW_BETA, h) for i, h in units]
    cumcols = [col(cums[i][0], ROW_G, h) for i, h in units]
    lasts = [col(cums[i][2], ROW_G, h) for i, h in units]
    decays = [jnp.where(masks.lower,
                        jnp.exp(jnp.where(masks.lower, cumcols[u] - cums[i][1][ROW_G + h:ROW_G + h + 1, :], 0.0)), 0.0)
              for u, (i, h) in enumerate(units)]
    qs, ks, vs = [], [], []
    for i, h in units:
        rows = slice(i * sub, (i + 1) * sub)
        q = ybuf[rows, h * GDN_KEY_DIM:(h + 1) * GDN_KEY_DIM]
        k = ybuf[rows, GDN_QK_WIDTH + h * GDN_KEY_DIM:GDN_QK_WIDTH + (h + 1) * GDN_KEY_DIM]
        qs.append(q * lax.rsqrt(jnp.sum(q * q, axis=-1, keepdims=True) + NORM_EPS) * (GDN_KEY_DIM ** -0.5))
        ks.append(k * lax.rsqrt(jnp.sum(k * k, axis=-1, keepdims=True) + NORM_EPS))
        vs.append(ybuf[rows, 2 * GDN_QK_WIDTH + h * GDN_VAL_DIM:2 * GDN_QK_WIDTH + (h + 1) * GDN_VAL_DIM])
    n = range(len(units))
    k_betas = [ks[u] * betas[u] for u in n]
    e_cums = [jnp.exp(cumcols[u]) for u in n]
    k_bfs = [ks[u].astype(BF16) for u in n]
    amats = [jnp.where(masks.strict, _dot_nt(k_betas[u].astype(BF16), k_bfs[u]) * decays[u], 0.0) for u in n]
    tmats = _unit_lower_inverses(amats, masks.same_block, masks.chunk, base=SUBLANES, dot=_dot1)
    uws = [_dot(tmats[u].astype(BF16),
                jnp.concatenate([vs[u] * betas[u], k_betas[u] * e_cums[u]], axis=-1).astype(BF16)) for u in n]
    qks = [(_dot_nt(qs[u].astype(BF16), k_bfs[u]) * decays[u]).astype(BF16) for u in n]
    flat = [(uws[u][:, :GDN_VAL_DIM], uws[u][:, GDN_VAL_DIM:].astype(BF16), qks[u], (qs[u] * e_cums[u]).astype(BF16),
             (ks[u] * jnp.exp(lasts[u] - cumcols[u])).astype(BF16), jnp.exp(lasts[u])) for u in n]
    return [flat[i * GDN_HEADS:(i + 1) * GDN_HEADS] for i in range(len(sblks))]


def _gated_out_norm(o, gz, gnorm):
    o = o * lax.rsqrt(jnp.mean(o * o, axis=-1, keepdims=True) + NORM_EPS) * gnorm
    return o * (gz * _sigmoid(gz))


def _gdn_prompt_kernel(gqkv_ref, s_ref, gz_ref, convw_ref, conv0_ref, s0_ref, gnorm_ref, o_ref, sfin_ref,
                       xbuf, ybuf, state, *, t_len, chunk):
    t = pl.program_id(1)

    @pl.when(t == 0)
    def _():
        xbuf[0:SUBLANES, :] = conv0_ref[0]
        state[...] = s0_ref[0]

    xbuf[SUBLANES:SUBLANES + t_len, :] = gqkv_ref[...]
    ybuf[...] = _short_conv(xbuf, convw_ref[...], t_len)
    xbuf[0:SUBLANES, :] = xbuf[t_len:t_len + SUBLANES, :]

    sub = min(t_len, LANES)
    masks = _ChunkMasks(sub, chunk)
    sblks = [s_ref[i * sub:(i + 1) * sub, :] for i in range(t_len // sub)]
    terms = _delta_rule_chunk_terms(ybuf, sblks, [_chunk_cumsums(sb, masks) for sb in sblks], masks)
    outs = [[] for _ in range(GDN_HEADS)]
    for sub_terms in terms:
        for c in range(sub // chunk):
            rows = slice(c * chunk, (c + 1) * chunk)
            for h, (u, wmat, qk, q_dec, k_dec, chunk_decay) in enumerate(sub_terms):
                st = state[h]
                ws = _dot(jnp.concatenate([wmat[rows], q_dec[rows]], axis=0), st.astype(BF16))
                v_new_bf = (u[rows] - ws[:chunk]).astype(BF16)
                outs[h].append(ws[chunk:] + _dot(qk[rows, rows], v_new_bf))
                state[h] = st * chunk_decay[c * chunk:c * chunk + 1] + _dot_tn(k_dec[rows], v_new_bf)
    for h in range(GDN_HEADS):
        head_cols = slice(h * GDN_VAL_DIM, (h + 1) * GDN_VAL_DIM)
        o_ref[:, head_cols] = _gated_out_norm(jnp.concatenate(outs[h], axis=0), gz_ref[:, head_cols],
                                              gnorm_ref[...]).astype(o_ref.dtype)

    @pl.when(t == pl.num_programs(1) - 1)
    def _():
        sfin_ref[0] = state[...]


def _gdn_prompt(gqkv, s, gz, conv_w, conv0, s0, gnorm, n_seq, t_len, chunk):
    n = gqkv.shape[0]
    tiles = n // n_seq // t_len
    rows = lambda width: pl.BlockSpec((t_len, width), lambda b, t: (b * tiles + t, 0))
    const = lambda shape: pl.BlockSpec(shape, lambda b, t: (0,) * len(shape))
    state_spec = pl.BlockSpec((1, GDN_HEADS, GDN_KEY_DIM, GDN_VAL_DIM), lambda b, t: (b, 0, 0, 0))
    return pl.pallas_call(
        functools.partial(_gdn_prompt_kernel, t_len=t_len, chunk=chunk),
        grid=(n_seq, tiles),
        in_specs=[rows(GDN_CONV_CH), rows(LANES), rows(GDN_V_WIDTH), const((GDN_CONV_WIDTH, GDN_CONV_CH)),
                  pl.BlockSpec((1, SUBLANES, GDN_CONV_CH), lambda b, t: (b, 0, 0)), state_spec,
                  const((1, GDN_VAL_DIM))],
        out_specs=(rows(GDN_V_WIDTH), state_spec),
        out_shape=(jax.ShapeDtypeStruct((n, GDN_V_WIDTH), BF16),
                   jax.ShapeDtypeStruct((n_seq, GDN_HEADS, GDN_KEY_DIM, GDN_VAL_DIM), F32)),
        scratch_shapes=[pltpu.VMEM((t_len + SUBLANES, GDN_CONV_CH), F32), pltpu.VMEM((t_len, GDN_CONV_CH), F32),
                        pltpu.VMEM((GDN_HEADS, GDN_KEY_DIM, GDN_VAL_DIM), F32)],
        compiler_params=pltpu.CompilerParams(dimension_semantics=("arbitrary", "arbitrary"),
                                             vmem_limit_bytes=VMEM_LIMIT),
        name="gated_deltanet_prompt",
    )(gqkv, s, gz, conv_w, conv0, s0, gnorm)


def _gdn_sample_kernel(xin_ref, s_ref, gz_ref, convw_ref, s0_ref, gnorm_ref, o_ref, sfin_ref, xbuf, ybuf,
                       *, n_seq, chunk):
    per_seq = SUBLANES + chunk
    n_in = n_seq * per_seq
    t_len = n_seq * chunk
    xbuf[0:SUBLANES, :] = jnp.zeros((SUBLANES, GDN_CONV_CH), F32)
    xbuf[SUBLANES:SUBLANES + n_in, :] = xin_ref[...]
    y = _short_conv(xbuf, convw_ref[...], n_in)
    for i in range(n_seq):
        ybuf[i * chunk:(i + 1) * chunk, :] = y[i * per_seq + SUBLANES:(i + 1) * per_seq]

    masks = _ChunkMasks(t_len, chunk)
    sblk = s_ref[...]
    cums = _chunk_cumsums(sblk, masks)
    seq_of_row = _div_pow2(_iota((t_len, GDN_KEY_DIM), 0), chunk)
    terms = _delta_rule_chunk_terms(ybuf, [sblk], [cums], masks)[0]
    for h, (u, wmat, qk, q_dec, k_dec, chunk_decay) in enumerate(terms):
        v_news, reads = [], []
        for i in range(n_seq):
            rows = slice(i * chunk, (i + 1) * chunk)
            ws = _dot(jnp.concatenate([wmat[rows], q_dec[rows]], axis=0), s0_ref[i, h].astype(BF16))
            v_news.append(u[rows] - ws[:chunk])
            reads.append(ws[chunk:])
        v_new_bf = jnp.concatenate(v_news, axis=0).astype(BF16)
        o = jnp.concatenate(reads, axis=0) + _dot(qk, v_new_bf)
        for i in range(n_seq):
            own_rows = jnp.where(seq_of_row == i, k_dec, jnp.zeros_like(k_dec))
            sfin_ref[i, h] = (s0_ref[i, h] * chunk_decay[i * chunk:i * chunk + 1]
                              + _dot_tn(own_rows, v_new_bf))
        head_cols = slice(h * GDN_VAL_DIM, (h + 1) * GDN_VAL_DIM)
        o_ref[:, head_cols] = _gated_out_norm(o, gz_ref[:, head_cols], gnorm_ref[...])


def _gdn_sample(xin, s, gz, conv_w, s0, gnorm, n_seq_total, seq_per_step, chunk):
    steps = n_seq_total // seq_per_step
    per_seq = SUBLANES + chunk
    t_len = seq_per_step * chunk
    rows = lambda width: pl.BlockSpec((t_len, width), lambda i: (i, 0))
    const = lambda shape: pl.BlockSpec(shape, lambda i: (0,) * len(shape))
    state_spec = pl.BlockSpec((seq_per_step, GDN_HEADS, GDN_KEY_DIM, GDN_VAL_DIM), lambda i: (i, 0, 0, 0))
    return pl.pallas_call(
        functools.partial(_gdn_sample_kernel, n_seq=seq_per_step, chunk=chunk),
        grid=(steps,),
        in_specs=[pl.BlockSpec((seq_per_step * per_seq, GDN_CONV_CH), lambda i: (i, 0)), rows(LANES),
                  rows(GDN_V_WIDTH), const((GDN_CONV_WIDTH, GDN_CONV_CH)), state_spec, const((1, GDN_VAL_DIM))],
        out_specs=(rows(GDN_V_WIDTH), state_spec),
        out_shape=(jax.ShapeDtypeStruct((n_seq_total * chunk, GDN_V_WIDTH), F32),
                   jax.ShapeDtypeStruct((n_seq_total, GDN_HEADS, GDN_KEY_DIM, GDN_VAL_DIM), F32)),
        scratch_shapes=[pltpu.VMEM((seq_per_step * per_seq + SUBLANES, GDN_CONV_CH), F32),
                        pltpu.VMEM((t_len, GDN_CONV_CH), F32)],
        compiler_params=pltpu.CompilerParams(dimension_semantics=("arbitrary",), vmem_limit_bytes=VMEM_LIMIT),
        name="gated_deltanet_sample",
    )(xin, s, gz, conv_w, s0, gnorm)


def _post_kernel(x_ref, oa_ref, ob_ref, gates_ref, ple_ref, wa_ref, wb_ref, wout_ref, gmlp_ref, wup_ref,
                 wdown_ref, gple_ref, wpg_ref, wple_ref, y_ref):
    def rms(a, g):
        return a * lax.rsqrt(jnp.mean(a * a, axis=-1, keepdims=True) + NORM_EPS) * g

    tm = x_ref.shape[0]
    n_sub = 1
    subs = [slice(i * (tm // n_sub), (i + 1) * (tm // n_sub)) for i in range(n_sub)]
    gates = [_sigmoid(gates_ref[r, :].astype(F32)) for r in subs]
    a = [_dot(oa_ref[r, :].astype(BF16), wa_ref[...]) for r in subs]
    b = [_dot(ob_ref[r, :].astype(BF16), wb_ref[...]) for r in subs]
    merged = [(g[:, :D_MODEL] * ai + g[:, D_MODEL:] * bi).astype(BF16) for g, ai, bi in zip(gates, a, b)]
    x = [x_ref[r, :] + _dot(m, wout_ref[...]) for r, m in zip(subs, merged)]
    h = [rms(xi, gmlp_ref[...]).astype(BF16) for xi in x]
    up = [jnp.maximum(_dot(hi, wup_ref[...]), 0.0) for hi in h]
    x = [xi + _dot((u * u).astype(BF16), wdown_ref[...]) for xi, u in zip(x, up)]
    hp = [rms(xi, gple_ref[...]).astype(BF16) for xi in x]
    ple_gate = [_sigmoid(_dot(hi, wpg_ref[...])) for hi in hp]
    for r, xi, g in zip(subs, x, ple_gate):
        y_ref[r, :] = xi + g * _dot(ple_ref[r, :].astype(BF16), wple_ref[...])


def _post(x, oa, ob, gates, ple, w, tm):
    n = x.shape[0]
    rows = lambda width: pl.BlockSpec((tm, width), lambda i: (i, 0))
    resident = lambda shape: pl.BlockSpec(shape, lambda i: (0, 0), pipeline_mode=pl.Buffered(1))
    return pl.pallas_call(
        _post_kernel,
        grid=(n // tm,),
        in_specs=[rows(D_MODEL), rows(FOX_WIDTH), rows(GDN_V_WIDTH), rows(2 * D_MODEL), rows(PLE_DIM),
                  resident((FOX_WIDTH, D_MODEL)), resident((GDN_V_WIDTH, D_MODEL)), resident((D_MODEL, D_MODEL)),
                  resident((1, D_MODEL)), resident((D_MODEL, D_FF)), resident((D_FF, D_MODEL)),
                  resident((1, D_MODEL)), resident((D_MODEL, D_MODEL)), resident((PLE_DIM, D_MODEL))],
        out_specs=rows(D_MODEL),
        out_shape=jax.ShapeDtypeStruct((n, D_MODEL), F32),
        compiler_params=pltpu.CompilerParams(dimension_semantics=("arbitrary",), vmem_limit_bytes=VMEM_LIMIT),
        name="merge_mlp_ple",
    )(x, oa, ob, gates, ple, w["w_a"], w["w_b"], w["w_out"], w["gmlp"], w["w_up"], w["w_down"], w["gple"],
      w["w_pg"], w["w_ple"])


def _prepare_weights(l, norm_mix_g, w_in, fox_f_bias, fox_q_norm_g, fox_k_norm_g, gdn_conv_w, gdn_a_log, gdn_dt_bias,
                     gdn_out_norm_g, w_branch_a, w_branch_b, w_out, norm_mlp_g, w_up, w_down, norm_ple_g,
                     w_ple_gate, w_ple):
    wi = w_in[l]
    o_ff = 3 * FOX_WIDTH
    o_gqkv = o_ff + FOX_HEADS
    o_ga = o_gqkv + GDN_CONV_CH
    o_gb = o_ga + GDN_HEADS
    o_gz = o_gb + GDN_HEADS
    o_gates = o_gz + GDN_V_WIDTH
    w_main = jnp.concatenate([wi[:, :o_ff], wi[:, o_gqkv:o_ga], wi[:, o_gz:]], axis=1).astype(BF16)
    w_small = jnp.concatenate([wi[:, o_ff:o_gqkv], wi[:, o_ga:o_gz]], axis=1)
    w_small = jnp.pad(w_small, ((0, 0), (0, LANES - N_SMALL))).astype(BF16)
    pad_col = lambda parts: jnp.pad(jnp.concatenate(parts), (0, LANES - N_SMALL)).reshape(LANES, 1).astype(F32)
    zeros_h = jnp.zeros((GDN_HEADS,), F32)
    head = jnp.arange(FOX_WIDTH) // FOX_HEAD_DIM
    return dict(
        gmix=norm_mix_g[l].reshape(1, D_MODEL),
        w_main=w_main,
        w_small=w_small,
        gmat=((head[:, None] == head[None, :]).astype(F32) / FOX_HEAD_DIM).astype(BF16),
        qg=jnp.tile(fox_q_norm_g[l], FOX_HEADS).reshape(1, FOX_WIDTH),
        kg=jnp.tile(fox_k_norm_g[l], FOX_HEADS).reshape(1, FOX_WIDTH),
        sbias=pad_col([fox_f_bias[l], gdn_dt_bias[l], zeros_h]),
        alog=pad_col([jnp.zeros((FOX_HEADS,), F32), gdn_a_log[l], zeros_h]),
        conv_w=gdn_conv_w[l],
        gnorm=gdn_out_norm_g[l].reshape(1, GDN_VAL_DIM),
        w_a=w_branch_a[l].astype(BF16),
        w_b=w_branch_b[l].astype(BF16),
        w_out=w_out[l].astype(BF16),
        gmlp=norm_mlp_g[l].reshape(1, D_MODEL),
        w_up=w_up[l].astype(BF16),
        w_down=w_down[l].astype(BF16),
        gple=norm_ple_g[l].reshape(1, D_MODEL),
        w_pg=w_ple_gate[l].astype(BF16),
        w_ple=w_ple[l].astype(BF16),
    )


def _pick_tile(n, target):
    t = min(n, target)
    while n % t:
        t //= 2
    return t


def _prompt_layer(x, ple, w):
    b, seq_len, _ = x.shape
    n = b * seq_len
    tm = _pick_tile(n, 512)
    q_bf, k_t, k_bf, v_t, v_bf, gqkv, gz, gates, s, st = _in_proj(
        x.reshape(n, D_MODEL), w, tm, FOX_HEAD_DIM ** -0.5 * LOG2E, kv_seq_len=seq_len)

    cum = _seq_cumsum(st, seq_len, LOG2E)
    ck = cum.reshape(FOX_HEADS, b, 1, seq_len).transpose(1, 0, 2, 3)
    tq = _pick_tile(seq_len, 1024)
    o_a = _fox_prompt(q_bf.reshape(b, seq_len, FOX_WIDTH), k_bf.reshape(b, seq_len, FOX_WIDTH),
                      v_bf.reshape(b, seq_len, FOX_WIDTH), ck, tq)

    chunk = math.gcd(seq_len, GDN_CHUNK)
    t_len = _pick_tile(seq_len, 8 * chunk)
    conv0 = jnp.zeros((b, SUBLANES, GDN_CONV_CH), F32)
    ssm0 = jnp.zeros((b, GDN_HEADS, GDN_KEY_DIM, GDN_VAL_DIM), F32)
    o_b, ssm = _gdn_prompt(gqkv, s, gz, w["conv_w"], conv0, ssm0, w["gnorm"], b, t_len, chunk)

    y = _post(x.reshape(n, D_MODEL), o_a.reshape(n, FOX_WIDTH), o_b, gates, ple.reshape(n, PLE_DIM), w,
              _pick_tile(n, 512))
    keep = GDN_CONV_WIDTH - 1
    token_major = lambda a: a.reshape(b, FOX_HEADS, FOX_HEAD_DIM, seq_len).transpose(0, 3, 1, 2)
    states = (token_major(k_t), token_major(v_t),
              st[ROW_LOGF:ROW_LOGF + FOX_HEADS].reshape(FOX_HEADS, b, seq_len).transpose(1, 2, 0),
              gqkv.reshape(b, seq_len, GDN_CONV_CH)[:, seq_len - keep:], ssm)
    return y.reshape(b, seq_len, D_MODEL), states


def _sample_layer(x, ple, w, k_pool, v_pool, lf_pool, conv_buf, ssm_state, page_table):
    b, s_new, _ = x.shape
    n = b * s_new
    keep = GDN_CONV_WIDTH - 1
    assert s_new >= keep and s_new <= SUBLANES
    tm = _pick_tile(n, 256)
    q_bf, k, k_bf, v, v_bf, gqkv, gz, gates, s, st = _in_proj(x.reshape(n, D_MODEL), w, tm, FOX_HEAD_DIM ** -0.5)

    n_pool, page = k_pool.shape[:2]
    q4 = q_bf.reshape(b, s_new, FOX_HEADS, 1, FOX_HEAD_DIM)
    eye = jnp.eye(FOX_HEADS, dtype=BF16).reshape(1, 1, FOX_HEADS, FOX_HEADS, 1)
    q_bd = (q4 * eye).reshape(b, s_new * FOX_HEADS, FOX_WIDTH)
    pad_keys = lambda a: jnp.pad(a.reshape(b, s_new, FOX_WIDTH).transpose(0, 2, 1),
                                 ((0, 0), (0, 0), (0, LANES - s_new)))
    lf_new_t = jnp.pad(st[ROW_LOGF:ROW_LOGF + FOX_HEADS].reshape(FOX_HEADS, b, s_new).transpose(1, 0, 2),
                       ((0, 0), (0, 0), (0, LANES - s_new)))
    pool_t = lambda a: a.transpose(0, 2, 3, 1).reshape(n_pool, FOX_WIDTH, page)
    o_a = _fox_sample(page_table, q_bd, pool_t(k_pool), pool_t(v_pool), lf_pool.transpose(0, 2, 1),
                      pad_keys(k_bf), pad_keys(v_bf), lf_new_t, pages_per_step=_pick_tile(page_table.shape[1], 32))

    chunk = SUBLANES
    pad_tok = lambda a: jnp.pad(a.reshape(b, s_new, -1), ((0, 0), (0, chunk - s_new), (0, 0)))
    xin = jnp.concatenate([jnp.pad(conv_buf, ((0, 0), (SUBLANES - keep, 0), (0, 0))), pad_tok(gqkv)], axis=1)
    seq_per_step = _pick_tile(b, LANES // chunk)
    o_b, ssm = _gdn_sample(xin.reshape(b * (SUBLANES + chunk), GDN_CONV_CH), pad_tok(s).reshape(b * chunk, LANES),
                           pad_tok(gz).reshape(b * chunk, GDN_V_WIDTH), w["conv_w"], ssm_state, w["gnorm"],
                           b, seq_per_step, chunk)
    o_b = o_b.reshape(b, chunk, GDN_V_WIDTH)[:, :s_new].reshape(n, GDN_V_WIDTH)

    y = _post(x.reshape(n, D_MODEL), o_a.reshape(n, FOX_WIDTH), o_b, gates, ple.reshape(n, PLE_DIM), w, tm)
    states = (k.reshape(b, s_new, FOX_HEADS, FOX_HEAD_DIM), v.reshape(b, s_new, FOX_HEADS, FOX_HEAD_DIM),
              s[:, ROW_LOGF:ROW_LOGF + FOX_HEADS].reshape(b, s_new, FOX_HEADS),
              gqkv.reshape(b, s_new, GDN_CONV_CH)[:, s_new - keep:], ssm)
    return y.reshape(b, s_new, D_MODEL), states


def kernel(x_prompt, x_sample, p_prompt, p_sample, cache_k, cache_v, cache_logf, state_conv, state_ssm, page_table,
           norm_mix_g, w_in, fox_f_bias, fox_q_norm_g, fox_k_norm_g, gdn_conv_w, gdn_a_log, gdn_dt_bias,
           gdn_out_norm_g, w_branch_a, w_branch_b, w_out, norm_mlp_g, w_up, w_down, norm_ple_g, w_ple_gate, w_ple):
    depth = w_in.shape[0]
    y_prompt, y_sample = x_prompt, x_sample
    prompt_states, sample_states = [], []
    for l in range(depth):
        w = _prepare_weights(l, norm_mix_g, w_in, fox_f_bias, fox_q_norm_g, fox_k_norm_g, gdn_conv_w, gdn_a_log,
                             gdn_dt_bias, gdn_out_norm_g, w_branch_a, w_branch_b, w_out, norm_mlp_g, w_up, w_down,
                             norm_ple_g, w_ple_gate, w_ple)
        y_prompt, st_p = _prompt_layer(y_prompt, p_prompt[l], w)
        y_sample, st_s = _sample_layer(y_sample, p_sample[l], w, cache_k[l], cache_v[l], cache_logf[l],
                                       state_conv[l], state_ssm[l], page_table)
        prompt_states.append(st_p)
        sample_states.append(st_s)
    stack = lambda states, i: jnp.stack([st[i] for st in states])
    return ((y_prompt, y_sample) + tuple(stack(prompt_states, i) for i in range(5))
            + tuple(stack(sample_states, i) for i in range(5)))
```

```python
import functools
import math

import jax
import jax.numpy as jnp
from jax import lax
from jax.experimental import pallas as pl
from jax.experimental.pallas import tpu as pltpu

F32 = jnp.float32
BF16 = jnp.bfloat16

D_MODEL = 1024
FOX_HEADS = 8
FOX_HEAD_DIM = 64
FOX_WIDTH = FOX_HEADS * FOX_HEAD_DIM
GDN_HEADS = 4
GDN_KEY_DIM = 128
GDN_VAL_DIM = 128
GDN_QK_WIDTH = GDN_HEADS * GDN_KEY_DIM
GDN_V_WIDTH = GDN_HEADS * GDN_VAL_DIM
GDN_CONV_WIDTH = 4
GDN_CONV_CH = 2 * GDN_QK_WIDTH + GDN_V_WIDTH
GDN_CHUNK = 64
D_FF = 4 * D_MODEL
PLE_DIM = 256
NORM_EPS = 1e-6

LANES = 128
SUBLANES = 8
NEG_BIG = -1e30
LOG2E = 1.4426950408889634
ZERO_PROB_LOG2 = 152.0
NORM_SLACK = 1.0 + 2.0 ** -7
VMEM_LIMIT = 56 * 1024 * 1024

_MAIN_SEGS = (FOX_WIDTH, FOX_WIDTH, FOX_WIDTH, GDN_CONV_CH, GDN_V_WIDTH, 2 * D_MODEL)
_MAIN_OFFS = tuple(sum(_MAIN_SEGS[:i]) for i in range(len(_MAIN_SEGS) + 1))
D_MAIN = _MAIN_OFFS[-1]
ROW_LOGF = 0
ROW_G = FOX_HEADS
ROW_BETA = FOX_HEADS + GDN_HEADS
N_SMALL = FOX_HEADS + 2 * GDN_HEADS


def _dot(a, b):
    return jnp.dot(a, b, preferred_element_type=F32)


def _dot_nt(a, b):
    return lax.dot_general(a, b, (((1,), (1,)), ((), ())), preferred_element_type=F32)


def _dot_tn(a, b):
    return lax.dot_general(a, b, (((0,), (0,)), ((), ())), preferred_element_type=F32)


def _split(a, parts, axis):
    pieces = []
    for _ in range(parts - 1):
        p = a.astype(BF16).astype(F32)
        pieces.append(p)
        a = a - p
    pieces.append(a)
    return jnp.concatenate(pieces, axis=axis).astype(BF16)


def _dot3(a, b):
    m, n = a.shape[0], b.shape[1]
    r = _dot(_split(a, 2, 0), _split(b, 2, 1))
    return (r[:m, :n] + r[:m, n:]) + (r[m:, :n] + r[m:, n:])


def _dot_exact_lhs(a_bf, b):
    n = b.shape[1]
    r = _dot(a_bf, _split(b, 3, 1))
    return r[:, :n] + (r[:, n:2 * n] + r[:, 2 * n:])


def _dot_exact_rhs(a, b_bf):
    m = a.shape[0]
    r = _dot(_split(a, 3, 0), b_bf)
    return r[:m] + (r[m:2 * m] + r[2 * m:])


def _sigmoid(x):
    return 0.5 * jnp.tanh(0.5 * x) + 0.5


def _iota(shape, dim):
    return lax.broadcasted_iota(jnp.int32, shape, dim)


def _div_pow2(x, divisor):
    shift = divisor.bit_length() - 1
    assert divisor == 1 << shift
    return lax.shift_right_logical(x, shift)


def _mod_pow2(x, divisor):
    assert divisor & (divisor - 1) == 0
    return x & (divisor - 1)


def _in_proj_kernel(x_ref, gmix_ref, wmain_ref, wsmall_ref, gmat_ref, qg_ref, kg_ref, sbias_ref, alog_ref,
                    q_ref, k_ref, kb_ref, v_ref, vb_ref, gqkv_ref, gz_ref, gates_ref, s_ref, st_ref,
                    *, feature_major_kv, q_scale):
    x = x_ref[...]
    xn = x * lax.rsqrt(jnp.mean(x * x, axis=-1, keepdims=True) + NORM_EPS) * gmix_ref[...]
    xb = xn.astype(BF16)

    def proj(seg):
        return _dot(xb, wmain_ref[:, _MAIN_OFFS[seg]:_MAIN_OFFS[seg + 1]])

    z = _dot(xb, wsmall_ref[...]).T + sbias_ref[...]
    row = _iota(z.shape, 0)
    t = jnp.log1p(jnp.exp(-jnp.abs(z)))
    logf = jnp.minimum(z, 0.0) - t
    g = -jnp.exp(alog_ref[...]) * (jnp.maximum(z, 0.0) + t)
    beta = _sigmoid(z)
    res = jnp.where(row < ROW_G, logf, jnp.where(row < ROW_BETA, g, jnp.where(row < N_SMALL, beta, 0.0)))
    st_ref[...] = res[:N_SMALL]
    s_ref[...] = res.T

    gmat = gmat_ref[...]

    def head_norm(t, g):
        ms = _dot((t * t).astype(BF16), gmat)
        return t * lax.rsqrt(ms + NORM_EPS) * g

    q = head_norm(proj(0), qg_ref[...])
    q_ref[...] = (q * q_scale).astype(BF16)
    k = head_norm(proj(1), kg_ref[...])
    kb_ref[...] = k.astype(BF16)
    v = proj(2)
    vb_ref[...] = v.astype(BF16)
    if feature_major_kv:
        k_ref[0] = k.T
        v_ref[0] = v.T
    else:
        k_ref[...] = k
        v_ref[...] = v
    gqkv_ref[...] = proj(3)
    gz_ref[...] = proj(4)
    gates_ref[...] = proj(5).astype(gates_ref.dtype)


def _in_proj(x, w, tm, q_scale, kv_seq_len=None):
    n = x.shape[0]
    const = lambda i: (0, 0)
    rows = lambda width: pl.BlockSpec((tm, width), lambda i: (i, 0))
    resident = lambda shape: pl.BlockSpec(shape, const, pipeline_mode=pl.Buffered(1))
    if kv_seq_len is None:
        kv_shape, kv_spec = jax.ShapeDtypeStruct((n, FOX_WIDTH), F32), rows(FOX_WIDTH)
    else:
        tiles = kv_seq_len // tm
        kv_shape = jax.ShapeDtypeStruct((n // kv_seq_len, FOX_WIDTH, kv_seq_len), F32)
        kv_spec = pl.BlockSpec((1, FOX_WIDTH, tm), lambda i: (i // tiles, 0, i % tiles))
    out_shape = (
        jax.ShapeDtypeStruct((n, FOX_WIDTH), BF16),
        kv_shape,
        jax.ShapeDtypeStruct((n, FOX_WIDTH), BF16),
        kv_shape,
        jax.ShapeDtypeStruct((n, FOX_WIDTH), BF16),
        jax.ShapeDtypeStruct((n, GDN_CONV_CH), F32),
        jax.ShapeDtypeStruct((n, GDN_V_WIDTH), F32),
        jax.ShapeDtypeStruct((n, 2 * D_MODEL), BF16),
        jax.ShapeDtypeStruct((n, LANES), F32),
        jax.ShapeDtypeStruct((N_SMALL, n), F32),
    )
    out_specs = (
        rows(FOX_WIDTH), kv_spec, rows(FOX_WIDTH), kv_spec, rows(FOX_WIDTH),
        rows(GDN_CONV_CH), rows(GDN_V_WIDTH), rows(2 * D_MODEL), rows(LANES),
        pl.BlockSpec((N_SMALL, tm), lambda i: (0, i)),
    )
    in_specs = [
        rows(D_MODEL),
        resident((1, D_MODEL)),
        resident((D_MODEL, D_MAIN)),
        resident((D_MODEL, LANES)),
        resident((FOX_WIDTH, FOX_WIDTH)),
        resident((1, FOX_WIDTH)),
        resident((1, FOX_WIDTH)),
        resident((LANES, 1)),
        resident((LANES, 1)),
    ]
    return pl.pallas_call(
        functools.partial(_in_proj_kernel, feature_major_kv=kv_seq_len is not None, q_scale=q_scale),
        grid=(n // tm,),
        in_specs=in_specs,
        out_specs=out_specs,
        out_shape=out_shape,
        compiler_params=pltpu.CompilerParams(dimension_semantics=("arbitrary",), vmem_limit_bytes=VMEM_LIMIT),
        name="in_proj",
    )(x, w["gmix"], w["w_main"], w["w_small"], w["gmat"], w["qg"], w["kg"], w["sbias"], w["alog"])


def _cumsum_kernel(lf_ref, cum_ref, *, scale):
    n_chunks = lf_ref.shape[1] // LANES
    upper = (_iota((LANES, LANES), 0) <= _iota((LANES, LANES), 1)).astype(BF16)
    chunks = jnp.concatenate([lf_ref[:, c * LANES:(c + 1) * LANES] for c in range(n_chunks)], axis=0)
    local = _dot_exact_rhs(chunks, upper)
    carry = jnp.zeros((FOX_HEADS, 1), F32)
    for c in range(n_chunks):
        cs = carry + local[c * FOX_HEADS:(c + 1) * FOX_HEADS]
        cum_ref[:, c * LANES:(c + 1) * LANES] = cs * scale
        carry = cs[:, LANES - 1:LANES]


def _seq_cumsum(st, seq_len, scale):
    n = st.shape[1]
    spec = pl.BlockSpec((FOX_HEADS, seq_len), lambda b: (0, b))
    return pl.pallas_call(
        functools.partial(_cumsum_kernel, scale=scale),
        grid=(n // seq_len,),
        in_specs=[spec],
        out_specs=spec,
        out_shape=jax.ShapeDtypeStruct((FOX_HEADS, n), F32),
        compiler_params=pltpu.CompilerParams(dimension_semantics=("arbitrary",)),
        name="fox_cumsum",
    )(st)


def _fox_prompt_kernel(q_ref, k_ref, v_ref, ck_ref, o_ref, kmax_sc, ckmin_sc, *, tq):
    qi = pl.program_id(2)
    lane = _iota((tq, LANES), 1)
    first = lane < FOX_HEAD_DIM
    in_head = (first, jnp.logical_not(first))
    tile_lane = _iota((1, LANES), 1)
    heads = (0, 1)

    @pl.when(qi == 0)
    def _():
        def scan(c, carry):
            sq_max, ck_min = carry
            sl = pl.ds(pl.multiple_of(c * tq, tq), tq)
            kk = k_ref[0, sl, :].astype(F32)
            sq = kk * kk
            sq_max = tuple(jnp.maximum(sq_max[h], jnp.max(jnp.sum(jnp.where(in_head[h], sq, 0.0), axis=-1,
                                                                  keepdims=True), axis=0, keepdims=True))
                           for h in heads)
            ck_min = tuple(jnp.where(tile_lane == c, jnp.min(ck_ref[0, h, :, sl], axis=-1, keepdims=True), ck_min[h])
                           for h in heads)
            return sq_max, ck_min
        zero, zero_row = jnp.zeros((1, 1), F32), jnp.zeros((1, LANES), F32)
        sq_max, ck_min = lax.fori_loop(0, k_ref.shape[1] // tq, scan, ((zero, zero), (zero_row, zero_row)))
        for h in heads:
            kmax_sc[h] = jnp.broadcast_to(jnp.sqrt(sq_max[h]), kmax_sc.shape[1:])
            ckmin_sc[h] = jnp.broadcast_to(ck_min[h], ckmin_sc.shape[1:])

    q2 = q_ref[0]
    causal = _iota((tq, tq), 0) >= _iota((tq, tq), 1)
    qs, cqs, reach = [], [], []
    for h in heads:
        qh = jnp.where(in_head[h], q2, jnp.zeros_like(q2))
        cq_row = ck_ref[0, h, :, pl.ds(pl.multiple_of(qi * tq, tq), tq)]
        cq = jnp.broadcast_to(cq_row, (LANES, tq)).T[:, 0:1]
        qf = qh.astype(F32)
        q_norm = jnp.sqrt(jnp.sum(qf * qf, axis=-1, keepdims=True))
        qs.append(qh)
        cqs.append(cq)
        reach.append(q_norm * kmax_sc[h][0:1, 0:1] * NORM_SLACK + cq)

    def steps(hs, j, carries, diagonal):
        sl = pl.ds(pl.multiple_of(j * tq, tq), tq)
        k2 = k_ref[0, sl, :]
        v2 = v_ref[0, sl, :]
        ss = [(cqs[h] - ck_ref[0, h, :, sl]) + _dot_nt(qs[h], k2) for h in hs]
        if diagonal:
            ss = [jnp.where(causal, s, -jnp.inf) for s in ss]
        m_news = [jnp.maximum(c[0], jnp.max(s, axis=-1, keepdims=True)) for c, s in zip(carries, ss)]
        alphas = [jnp.exp2(c[0] - m_new) for c, m_new in zip(carries, m_news)]
        ps = [jnp.exp2(s - m_new) for s, m_new in zip(ss, m_news)]
        ls = [alpha * c[1] + jnp.sum(p, axis=-1, keepdims=True) for alpha, c, p in zip(alphas, carries, ps)]
        accs = [alpha * c[2] + _dot(p.astype(BF16), v2) for alpha, c, p in zip(alphas, carries, ps)]
        return tuple(zip(m_news, ls, accs))

    def tiles_reached(h, m):
        gap = jnp.max(reach[h] - m, axis=0, keepdims=True) - ckmin_sc[h][0:1, :]
        hit = jnp.logical_and(tile_lane < qi, gap >= -ZERO_PROB_LOG2)
        return jnp.sum(hit.astype(F32)).astype(jnp.int32)

    col = lambda val: jnp.full((tq, 1), val, F32)
    init = (col(NEG_BIG), col(0.0), jnp.zeros((tq, LANES), F32))
    ca, cb = steps(heads, qi, (init, init), diagonal=True)
    n_a, n_b = tiles_reached(0, ca[0]), tiles_reached(1, cb[0])
    n_both = jnp.minimum(n_a, n_b)
    ca, cb = lax.fori_loop(0, n_both, lambda i, c: steps(heads, qi - 1 - i, c, diagonal=False), (ca, cb))
    ca = lax.fori_loop(n_both, n_a, lambda i, c: steps((0,), qi - 1 - i, (c,), diagonal=False)[0], ca)
    cb = lax.fori_loop(n_both, n_b, lambda i, c: steps((1,), qi - 1 - i, (c,), diagonal=False)[0], cb)
    o_ref[0] = jnp.where(first, ca[2] * (1.0 / ca[1]), cb[2] * (1.0 / cb[1])).astype(o_ref.dtype)


def _fox_prompt(qb, kb, vb, ck, tq):
    b, seq_len, _ = qb.shape
    pairs = FOX_HEADS // 2
    return pl.pallas_call(
        functools.partial(_fox_prompt_kernel, tq=tq),
        grid=(b, pairs, seq_len // tq),
        in_specs=[
            pl.BlockSpec((1, tq, LANES), lambda b_, p, i: (b_, i, p)),
            pl.BlockSpec((1, seq_len, LANES), lambda b_, p, i: (b_, 0, p)),
            pl.BlockSpec((1, seq_len, LANES), lambda b_, p, i: (b_, 0, p)),
            pl.BlockSpec((1, 2, 1, seq_len), lambda b_, p, i: (b_, p, 0, 0)),
        ],
        out_specs=pl.BlockSpec((1, tq, LANES), lambda b_, p, i: (b_, i, p)),
        out_shape=jax.ShapeDtypeStruct((b, seq_len, FOX_WIDTH), BF16),
        scratch_shapes=[pltpu.VMEM((2, SUBLANES, LANES), F32), pltpu.VMEM((2, SUBLANES, LANES), F32)],
        compiler_params=pltpu.CompilerParams(dimension_semantics=("arbitrary", "arbitrary", "arbitrary"),
                                             vmem_limit_bytes=VMEM_LIMIT),
        name="fox_prompt_attention",
    )(qb, kb, vb, ck)


def _fox_sample_kernel(pt_ref, q_ref, *refs, pages_per_step, n_new):
    del pt_ref
    pp = pages_per_step
    k_refs, v_refs, lf_refs = refs[:pp], refs[pp:2 * pp], refs[2 * pp:3 * pp]
    knew_ref, vnew_ref, lfnew_ref, o_ref, m_sc, l_sc, acc_sc = refs[3 * pp:]
    step_id = pl.program_id(1)
    n_rows = n_new * FOX_HEADS

    @pl.when(step_id == 0)
    def _():
        m_sc[...] = jnp.full(m_sc.shape, NEG_BIG, F32)
        l_sc[...] = jnp.zeros(l_sc.shape, F32)
        acc_sc[...] = jnp.zeros(acc_sc.shape, F32)

    q = q_ref[0]
    ri = _iota((LANES, LANES), 0)
    ci = _iota((LANES, LANES), 1)
    after = (ri > ci).astype(BF16)
    upto = (ri <= ci).astype(BF16)
    tile_heads = lambda a: jnp.concatenate([a] * n_new, axis=0)

    def update(s, shift, v_t):
        m_prev = m_sc[...] + shift
        m_new = jnp.maximum(m_prev, jnp.max(s, axis=-1, keepdims=True))
        alpha = jnp.exp(m_prev - m_new)
        p = jnp.exp(s - m_new)
        l_sc[...] = alpha * l_sc[...] + jnp.sum(p, axis=-1, keepdims=True)
        acc_sc[...] = alpha * acc_sc[...] + _dot_nt(p.astype(BF16), v_t)
        m_sc[...] = m_new

    lf = jnp.concatenate([r[0] for r in lf_refs], axis=0)
    suffix = _dot_exact_rhs(lf, after)
    total = suffix[:, 0:1] + lf[:, 0:1]
    bias, later = [], jnp.zeros((FOX_HEADS, 1), F32)
    for j in reversed(range(pp)):
        rows = slice(j * FOX_HEADS, (j + 1) * FOX_HEADS)
        bias.append(tile_heads(suffix[rows] + later))
        later = later + total[rows]
    bias = jnp.concatenate(bias[::-1], axis=1)
    k_t = jnp.concatenate([r[0].astype(BF16) for r in k_refs], axis=1)
    v_t = jnp.concatenate([r[0].astype(BF16) for r in v_refs], axis=1)
    update(bias + _dot(q, k_t), tile_heads(later), v_t)

    @pl.when(step_id == pl.num_programs(1) - 1)
    def _():
        cum_rows = tile_heads(_dot_exact_rhs(lfnew_ref[0], upto))
        lane = _iota((n_rows, LANES), 1)
        query = _div_pow2(_iota((n_rows, LANES), 0), FOX_HEADS)
        cum_q = jnp.sum(jnp.where(lane == query, cum_rows, 0.0), axis=-1, keepdims=True)
        s = (cum_q - cum_rows) + _dot(q, knew_ref[0])
        s = jnp.where(lane <= query, s, -jnp.inf)
        update(s, cum_q, vnew_ref[0])
        out = acc_sc[...] / l_sc[...]
        own = _div_pow2(_iota(out.shape, 1), FOX_HEAD_DIM) == _mod_pow2(_iota(out.shape, 0), FOX_HEADS)
        out = jnp.where(own, out, 0.0)
        o_ref[0] = jnp.concatenate(
            [jnp.sum(out[i * FOX_HEADS:(i + 1) * FOX_HEADS], axis=0, keepdims=True) for i in range(n_new)], axis=0)


def _fox_sample(page_table, q_bd, k_pool_t, v_pool_t, lf_pool_t, k_new_t, v_new_t, lf_new_t, pages_per_step):
    n_seq, n_pages = page_table.shape
    n_rows = q_bd.shape[1]
    n_new = n_rows // FOX_HEADS
    page = k_pool_t.shape[2]
    pp = pages_per_step
    steps = n_pages // pp

    def paged(shape, j):
        return pl.BlockSpec(shape, lambda b, s, pt: (pt[b * n_pages + s * pp + j], 0, 0))

    per_seq = lambda shape: pl.BlockSpec(shape, lambda b, s, pt: (b, 0, 0))
    in_specs = ([per_seq((1, n_rows, FOX_WIDTH))]
                + [paged((1, FOX_WIDTH, page), j) for j in range(pp)]
                + [paged((1, FOX_WIDTH, page), j) for j in range(pp)]
                + [paged((1, FOX_HEADS, page), j) for j in range(pp)]
                + [per_seq((1, FOX_WIDTH, LANES)), per_seq((1, FOX_WIDTH, LANES)), per_seq((1, FOX_HEADS, LANES))])
    grid_spec = pltpu.PrefetchScalarGridSpec(
        num_scalar_prefetch=1,
        grid=(n_seq, steps),
        in_specs=in_specs,
        out_specs=per_seq((1, n_new, FOX_WIDTH)),
        scratch_shapes=[pltpu.VMEM((n_rows, 1), F32), pltpu.VMEM((n_rows, 1), F32),
                        pltpu.VMEM((n_rows, FOX_WIDTH), F32)],
    )
    return pl.pallas_call(
        functools.partial(_fox_sample_kernel, pages_per_step=pp, n_new=n_new),
        grid_spec=grid_spec,
        out_shape=jax.ShapeDtypeStruct((n_seq, n_new, FOX_WIDTH), F32),
        compiler_params=pltpu.CompilerParams(dimension_semantics=("arbitrary", "arbitrary"),
                                             vmem_limit_bytes=VMEM_LIMIT),
        name="fox_sample_attention",
    )(page_table.reshape(-1), q_bd, *([k_pool_t] * pp), *([v_pool_t] * pp), *([lf_pool_t] * pp),
      k_new_t, v_new_t, lf_new_t)


def _dot1(a, b):
    return _dot(a.astype(BF16), b.astype(BF16))


def _unit_lower_inverses(mats, same_block, chunk, base, dot):
    t_len = mats[0].shape[0]
    eye = (_iota((t_len, t_len), 0) == _iota((t_len, t_len), 1)).astype(F32)
    size = min(base, chunk)
    in_base = same_block(size)
    ns = [jnp.where(in_base, -a, 0.0) for a in mats]
    invs = [eye + n for n in ns]
    power = 1
    while 2 * power < size:
        ns = [dot(n, n) for n in ns]
        invs = [inv + dot(inv, n) for inv, n in zip(invs, ns)]
        power *= 2
    while size < chunk:
        off_mask = same_block(2 * size) & jnp.logical_not(same_block(size))
        corr = [dot(jnp.where(off_mask, a, 0.0), inv) for a, inv in zip(mats, invs)]
        invs = [inv - dot(inv, c) for inv, c in zip(invs, corr)]
        size *= 2
    return invs


def _short_conv(xbuf, conv_w, n_rows):
    first = SUBLANES - GDN_CONV_WIDTH + 1
    y = conv_w[0:1] * xbuf[first:first + n_rows, :]
    for i in range(1, GDN_CONV_WIDTH):
        y = y + conv_w[i:i + 1] * xbuf[first + i:first + i + n_rows, :]
    return y * _sigmoid(y)


class _ChunkMasks:
    def __init__(self, t_len, chunk):
        self.t_len, self.chunk = t_len, chunk
        ri = _iota((t_len, t_len), 0)
        ci = _iota((t_len, t_len), 1)
        self.same_block = lambda size: _div_pow2(ri, size) == _div_pow2(ci, size)
        in_chunk = self.same_block(chunk)
        self.lower = in_chunk & (ri >= ci)
        self.strict = in_chunk & (ri > ci)


def _chunk_cumsums(sblk, masks):
    chunk = masks.chunk
    cum_all = _dot_exact_lhs(masks.lower.astype(BF16), sblk)
    last_all = jnp.concatenate(
        [jnp.broadcast_to(cum_all[(c + 1) * chunk - 1:(c + 1) * chunk, :], (chunk, LANES))
         for c in range(masks.t_len // chunk)], axis=0)
    return cum_all, cum_all.T, last_all


def _delta_rule_chunk_terms(ybuf, sblks, cums, masks):
    sub = masks.t_len
    units = [(i, h) for i in range(len(sblks)) for h in range(GDN_HEADS)]
    col = lambda a, base, h: a[:, base + h:base + h + 1]
    betas = [col(sblks[i], ROW_BETA, h) for i, h in units]
    cumcols = [col(cums[i][0], ROW_G, h) for i, h in units]
    lasts = [col(cums[i][2], ROW_G, h) for i, h in units]
    decays = [jnp.where(masks.lower,
                        jnp.exp(jnp.where(masks.lower, cumcols[u] - cums[i][1][ROW_G + h:ROW_G + h + 1, :], 0.0)), 0.0)
              for u, (i, h) in enumerate(units)]
    qs, ks, vs = [], [], []
    for i, h in units:
        rows = slice(i * sub, (i + 1) * sub)
        q = ybuf[rows, h * GDN_KEY_DIM:(h + 1) * GDN_KEY_DIM]
        k = ybuf[rows, GDN_QK_WIDTH + h * GDN_KEY_DIM:GDN_QK_WIDTH + (h + 1) * GDN_KEY_DIM]
        qs.append(q * lax.rsqrt(jnp.sum(q * q, axis=-1, keepdims=True) + NORM_EPS) * (GDN_KEY_DIM ** -0.5))
        ks.append(k * lax.rsqrt(jnp.sum(k * k, axis=-1, keepdims=True) + NORM_EPS))
        vs.append(ybuf[rows, 2 * GDN_QK_WIDTH + h * GDN_VAL_DIM:2 * GDN_QK_WIDTH + (h + 1) * GDN_VAL_DIM])
    n = range(len(units))
    k_betas = [ks[u] * betas[u] for u in n]
    e_cums = [jnp.exp(cumcols[u]) for u in n]
    k_bfs = [ks[u].astype(BF16) for u in n]
    amats = [jnp.where(masks.strict, _dot_nt(k_betas[u].astype(BF16), k_bfs[u]) * decays[u], 0.0) for u in n]
    tmats = _unit_lower_inverses(amats, masks.same_block, masks.chunk, base=SUBLANES, dot=_dot1)
    uws = [_dot(tmats[u].astype(BF16),
                jnp.concatenate([vs[u] * betas[u], k_betas[u] * e_cums[u]], axis=-1).astype(BF16)) for u in n]
    qks = [(_dot_nt(qs[u].astype(BF16), k_bfs[u]) * decays[u]).astype(BF16) for u in n]
    flat = [(uws[u][:, :GDN_VAL_DIM], uws[u][:, GDN_VAL_DIM:].astype(BF16), qks[u], (qs[u] * e_cums[u]).astype(BF16),
             (ks[u] * jnp.exp(lasts[u] - cumcols[u])).astype(BF16), jnp.exp(lasts[u])) for u in n]
    return [flat[i * GDN_HEADS:(i + 1) * GDN_HEADS] for i in range(len(sblks))]


def _gated_out_norm(o, gz, gnorm):
    o = o * lax.rsqrt(jnp.mean(o * o, axis=-1, keepdims=True) + NORM_EPS) * gnorm
    return o * (gz * _sigmoid(gz))


def _gdn_prompt_kernel(gqkv_ref, s_ref, gz_ref, convw_ref, conv0_ref, s0_ref, gnorm_ref, o_ref, sfin_ref,
                       xbuf, ybuf, state, *, t_len, chunk):
    t = pl.program_id(1)

    @pl.when(t == 0)
    def _():
        xbuf[0:SUBLANES, :] = conv0_ref[0]
        state[...] = s0_ref[0]

    xbuf[SUBLANES:SUBLANES + t_len, :] = gqkv_ref[...]
    ybuf[...] = _short_conv(xbuf, convw_ref[...], t_len)
    xbuf[0:SUBLANES, :] = xbuf[t_len:t_len + SUBLANES, :]

    sub = min(t_len, LANES)
    masks = _ChunkMasks(sub, chunk)
    sblks = [s_ref[i * sub:(i + 1) * sub, :] for i in range(t_len // sub)]
    terms = _delta_rule_chunk_terms(ybuf, sblks, [_chunk_cumsums(sb, masks) for sb in sblks], masks)
    outs = [[] for _ in range(GDN_HEADS)]
    for sub_terms in terms:
        for c in range(sub // chunk):
            rows = slice(c * chunk, (c + 1) * chunk)
            for h, (u, wmat, qk, q_dec, k_dec, chunk_decay) in enumerate(sub_terms):
                st = state[h]
                ws = _dot(jnp.concatenate([wmat[rows], q_dec[rows]], axis=0), st.astype(BF16))
                v_new_bf = (u[rows] - ws[:chunk]).astype(BF16)
                outs[h].append(ws[chunk:] + _dot(qk[rows, rows], v_new_bf))
                state[h] = st * chunk_decay[c * chunk:c * chunk + 1] + _dot_tn(k_dec[rows], v_new_bf)
    for h in range(GDN_HEADS):
        head_cols = slice(h * GDN_VAL_DIM, (h + 1) * GDN_VAL_DIM)
        o_ref[:, head_cols] = _gated_out_norm(jnp.concatenate(outs[h], axis=0), gz_ref[:, head_cols],
                                              gnorm_ref[...]).astype(o_ref.dtype)

    @pl.when(t == pl.num_programs(1) - 1)
    def _():
        sfin_ref[0] = state[...]


def _gdn_prompt(gqkv, s, gz, conv_w, conv0, s0, gnorm, n_seq, t_len, chunk):
    n = gqkv.shape[0]
    tiles = n // n_seq // t_len
    rows = lambda width: pl.BlockSpec((t_len, width), lambda b, t: (b * tiles + t, 0))
    const = lambda shape: pl.BlockSpec(shape, lambda b, t: (0,) * len(shape))
    state_spec = pl.BlockSpec((1, GDN_HEADS, GDN_KEY_DIM, GDN_VAL_DIM), lambda b, t: (b, 0, 0, 0))
    return pl.pallas_call(
        functools.partial(_gdn_prompt_kernel, t_len=t_len, chunk=chunk),
        grid=(n_seq, tiles),
        in_specs=[rows(GDN_CONV_CH), rows(LANES), rows(GDN_V_WIDTH), const((GDN_CONV_WIDTH, GDN_CONV_CH)),
                  pl.BlockSpec((1, SUBLANES, GDN_CONV_CH), lambda b, t: (b, 0, 0)), state_spec,
                  const((1, GDN_VAL_DIM))],
        out_specs=(rows(GDN_V_WIDTH), state_spec),
        out_shape=(jax.ShapeDtypeStruct((n, GDN_V_WIDTH), BF16),
                   jax.ShapeDtypeStruct((n_seq, GDN_HEADS, GDN_KEY_DIM, GDN_VAL_DIM), F32)),
        scratch_shapes=[pltpu.VMEM((t_len + SUBLANES, GDN_CONV_CH), F32), pltpu.VMEM((t_len, GDN_CONV_CH), F32),
                        pltpu.VMEM((GDN_HEADS, GDN_KEY_DIM, GDN_VAL_DIM), F32)],
        compiler_params=pltpu.CompilerParams(dimension_semantics=("arbitrary", "arbitrary"),
                                             vmem_limit_bytes=VMEM_LIMIT),
        name="gated_deltanet_prompt",
    )(gqkv, s, gz, conv_w, conv0, s0, gnorm)


def _gdn_sample_kernel(xin_ref, s_ref, gz_ref, convw_ref, s0_ref, gnorm_ref, o_ref, sfin_ref, xbuf, ybuf,
                       *, n_seq, chunk):
    per_seq = SUBLANES + chunk
    n_in = n_seq * per_seq
    t_len = n_seq * chunk
    xbuf[0:SUBLANES, :] = jnp.zeros((SUBLANES, GDN_CONV_CH), F32)
    xbuf[SUBLANES:SUBLANES + n_in, :] = xin_ref[...]
    y = _short_conv(xbuf, convw_ref[...], n_in)
    for i in range(n_seq):
        ybuf[i * chunk:(i + 1) * chunk, :] = y[i * per_seq + SUBLANES:(i + 1) * per_seq]

    masks = _ChunkMasks(t_len, chunk)
    sblk = s_ref[...]
    cums = _chunk_cumsums(sblk, masks)
    seq_of_row = _div_pow2(_iota((t_len, GDN_KEY_DIM), 0), chunk)
    terms = _delta_rule_chunk_terms(ybuf, [sblk], [cums], masks)[0]
    for h, (u, wmat, qk, q_dec, k_dec, chunk_decay) in enumerate(terms):
        v_news, reads = [], []
        for i in range(n_seq):
            rows = slice(i * chunk, (i + 1) * chunk)
            ws = _dot(jnp.concatenate([wmat[rows], q_dec[rows]], axis=0), s0_ref[i, h].astype(BF16))
            v_news.append(u[rows] - ws[:chunk])
            reads.append(ws[chunk:])
        v_new_bf = jnp.concatenate(v_news, axis=0).astype(BF16)
        o = jnp.concatenate(reads, axis=0) + _dot(qk, v_new_bf)
        for i in range(n_seq):
            own_rows = jnp.where(seq_of_row == i, k_dec, jnp.zeros_like(k_dec))
            sfin_ref[i, h] = (s0_ref[i, h] * chunk_decay[i * chunk:i * chunk + 1]
                              + _dot_tn(own_rows, v_new_bf))
        head_cols = slice(h * GDN_VAL_DIM, (h + 1) * GDN_VAL_DIM)
        o_ref[:, head_cols] = _gated_out_norm(o, gz_ref[:, head_cols], gnorm_ref[...])


def _gdn_sample(xin, s, gz, conv_w, s0, gnorm, n_seq_total, seq_per_step, chunk):
    steps = n_seq_total // seq_per_step
    per_seq = SUBLANES + chunk
    t_len = seq_per_step * chunk
    rows = lambda width: pl.BlockSpec((t_len, width), lambda i: (i, 0))
    const = lambda shape: pl.BlockSpec(shape, lambda i: (0,) * len(shape))
    state_spec = pl.BlockSpec((seq_per_step, GDN_HEADS, GDN_KEY_DIM, GDN_VAL_DIM), lambda i: (i, 0, 0, 0))
    return pl.pallas_call(
        functools.partial(_gdn_sample_kernel, n_seq=seq_per_step, chunk=chunk),
        grid=(steps,),
        in_specs=[pl.BlockSpec((seq_per_step * per_seq, GDN_CONV_CH), lambda i: (i, 0)), rows(LANES),
                  rows(GDN_V_WIDTH), const((GDN_CONV_WIDTH, GDN_CONV_CH)), state_spec, const((1, GDN_VAL_DIM))],
        out_specs=(rows(GDN_V_WIDTH), state_spec),
        out_shape=(jax.ShapeDtypeStruct((n_seq_total * chunk, GDN_V_WIDTH), F32),
                   jax.ShapeDtypeStruct((n_seq_total, GDN_HEADS, GDN_KEY_DIM, GDN_VAL_DIM), F32)),
        scratch_shapes=[pltpu.VMEM((seq_per_step * per_seq + SUBLANES, GDN_CONV_CH), F32),
                        pltpu.VMEM((t_len, GDN_CONV_CH), F32)],
        compiler_params=pltpu.CompilerParams(dimension_semantics=("arbitrary",), vmem_limit_bytes=VMEM_LIMIT),
        name="gated_deltanet_sample",
    )(xin, s, gz, conv_w, s0, gnorm)


def _post_kernel(x_ref, oa_ref, ob_ref, gates_ref, ple_ref, wa_ref, wb_ref, wout_ref, gmlp_ref, wup_ref,
                 wdown_ref, gple_ref, wpg_ref, wple_ref, y_ref):
    def rms(a, g):
        return a * lax.rsqrt(jnp.mean(a * a, axis=-1, keepdims=True) + NORM_EPS) * g

    gates = _sigmoid(gates_ref[...].astype(F32))
    merged = (gates[:, :D_MODEL] * _dot(oa_ref[...].astype(BF16), wa_ref[...])
              + gates[:, D_MODEL:] * _dot(ob_ref[...].astype(BF16), wb_ref[...]))
    x = x_ref[...] + _dot(merged.astype(BF16), wout_ref[...])
    up = jnp.maximum(_dot(rms(x, gmlp_ref[...]).astype(BF16), wup_ref[...]), 0.0)
    x = x + _dot((up * up).astype(BF16), wdown_ref[...])
    ple_gate = _sigmoid(_dot(rms(x, gple_ref[...]).astype(BF16), wpg_ref[...]))
    y_ref[...] = x + ple_gate * _dot(ple_ref[...].astype(BF16), wple_ref[...])


def _post(x, oa, ob, gates, ple, w, tm):
    n = x.shape[0]
    rows = lambda width: pl.BlockSpec((tm, width), lambda i: (i, 0))
    resident = lambda shape: pl.BlockSpec(shape, lambda i: (0, 0), pipeline_mode=pl.Buffered(1))
    return pl.pallas_call(
        _post_kernel,
        grid=(n // tm,),
        in_specs=[rows(D_MODEL), rows(FOX_WIDTH), rows(GDN_V_WIDTH), rows(2 * D_MODEL), rows(PLE_DIM),
                  resident((FOX_WIDTH, D_MODEL)), resident((GDN_V_WIDTH, D_MODEL)), resident((D_MODEL, D_MODEL)),
                  resident((1, D_MODEL)), resident((D_MODEL, D_FF)), resident((D_FF, D_MODEL)),
                  resident((1, D_MODEL)), resident((D_MODEL, D_MODEL)), resident((PLE_DIM, D_MODEL))],
        out_specs=rows(D_MODEL),
        out_shape=jax.ShapeDtypeStruct((n, D_MODEL), F32),
        compiler_params=pltpu.CompilerParams(dimension_semantics=("arbitrary",), vmem_limit_bytes=VMEM_LIMIT),
        name="merge_mlp_ple",
    )(x, oa, ob, gates, ple, w["w_a"], w["w_b"], w["w_out"], w["gmlp"], w["w_up"], w["w_down"], w["gple"],
      w["w_pg"], w["w_ple"])


def _prepare_weights(l, norm_mix_g, w_in, fox_f_bias, fox_q_norm_g, fox_k_norm_g, gdn_conv_w, gdn_a_log, gdn_dt_bias,
                     gdn_out_norm_g, w_branch_a, w_branch_b, w_out, norm_mlp_g, w_up, w_down, norm_ple_g,
                     w_ple_gate, w_ple):
    wi = w_in[l]
    o_ff = 3 * FOX_WIDTH
    o_gqkv = o_ff + FOX_HEADS
    o_ga = o_gqkv + GDN_CONV_CH
    o_gb = o_ga + GDN_HEADS
    o_gz = o_gb + GDN_HEADS
    o_gates = o_gz + GDN_V_WIDTH
    w_main = jnp.concatenate([wi[:, :o_ff], wi[:, o_gqkv:o_ga], wi[:, o_gz:]], axis=1).astype(BF16)
    w_small = jnp.concatenate([wi[:, o_ff:o_gqkv], wi[:, o_ga:o_gz]], axis=1)
    w_small = jnp.pad(w_small, ((0, 0), (0, LANES - N_SMALL))).astype(BF16)
    pad_col = lambda parts: jnp.pad(jnp.concatenate(parts), (0, LANES - N_SMALL)).reshape(LANES, 1).astype(F32)
    zeros_h = jnp.zeros((GDN_HEADS,), F32)
    head = jnp.arange(FOX_WIDTH) // FOX_HEAD_DIM
    return dict(
        gmix=norm_mix_g[l].reshape(1, D_MODEL),
        w_main=w_main,
        w_small=w_small,
        gmat=((head[:, None] == head[None, :]).astype(F32) / FOX_HEAD_DIM).astype(BF16),
        qg=jnp.tile(fox_q_norm_g[l], FOX_HEADS).reshape(1, FOX_WIDTH),
        kg=jnp.tile(fox_k_norm_g[l], FOX_HEADS).reshape(1, FOX_WIDTH),
        sbias=pad_col([fox_f_bias[l], gdn_dt_bias[l], zeros_h]),
        alog=pad_col([jnp.zeros((FOX_HEADS,), F32), gdn_a_log[l], zeros_h]),
        conv_w=gdn_conv_w[l],
        gnorm=gdn_out_norm_g[l].reshape(1, GDN_VAL_DIM),
        w_a=w_branch_a[l].astype(BF16),
        w_b=w_branch_b[l].astype(BF16),
        w_out=w_out[l].astype(BF16),
        gmlp=norm_mlp_g[l].reshape(1, D_MODEL),
        w_up=w_up[l].astype(BF16),
        w_down=w_down[l].astype(BF16),
        gple=norm_ple_g[l].reshape(1, D_MODEL),
        w_pg=w_ple_gate[l].astype(BF16),
        w_ple=w_ple[l].astype(BF16),
    )


ROWS_PER_STEP = 512
SAMPLE_ROWS_PER_STEP = 256
ATTN_TILE = 1024
GDN_CHUNKS_PER_STEP = 8
PAGES_PER_STEP = 32


def _pick_tile(n, target):
    t = min(n, target)
    while n % t:
        t //= 2
    return t


def _prompt_layer(x, ple, w):
    b, seq_len, _ = x.shape
    n = b * seq_len
    tm = _pick_tile(n, ROWS_PER_STEP)
    q_bf, k_t, k_bf, v_t, v_bf, gqkv, gz, gates, s, st = _in_proj(
        x.reshape(n, D_MODEL), w, tm, FOX_HEAD_DIM ** -0.5 * LOG2E, kv_seq_len=seq_len)

    cum = _seq_cumsum(st, seq_len, LOG2E)
    ck = cum.reshape(FOX_HEADS, b, 1, seq_len).transpose(1, 0, 2, 3)
    tq = _pick_tile(seq_len, ATTN_TILE)
    o_a = _fox_prompt(q_bf.reshape(b, seq_len, FOX_WIDTH), k_bf.reshape(b, seq_len, FOX_WIDTH),
                      v_bf.reshape(b, seq_len, FOX_WIDTH), ck, tq)

    chunk = math.gcd(seq_len, GDN_CHUNK)
    t_len = _pick_tile(seq_len, GDN_CHUNKS_PER_STEP * chunk)
    conv0 = jnp.zeros((b, SUBLANES, GDN_CONV_CH), F32)
    ssm0 = jnp.zeros((b, GDN_HEADS, GDN_KEY_DIM, GDN_VAL_DIM), F32)
    o_b, ssm = _gdn_prompt(gqkv, s, gz, w["conv_w"], conv0, ssm0, w["gnorm"], b, t_len, chunk)

    y = _post(x.reshape(n, D_MODEL), o_a.reshape(n, FOX_WIDTH), o_b, gates, ple.reshape(n, PLE_DIM), w,
              tm)
    keep = GDN_CONV_WIDTH - 1
    token_major = lambda a: a.reshape(b, FOX_HEADS, FOX_HEAD_DIM, seq_len).transpose(0, 3, 1, 2)
    states = (token_major(k_t), token_major(v_t),
              st[ROW_LOGF:ROW_LOGF + FOX_HEADS].reshape(FOX_HEADS, b, seq_len).transpose(1, 2, 0),
              gqkv.reshape(b, seq_len, GDN_CONV_CH)[:, seq_len - keep:], ssm)
    return y.reshape(b, seq_len, D_MODEL), states


def _sample_layer(x, ple, w, k_pool, v_pool, lf_pool, conv_buf, ssm_state, page_table):
    b, s_new, _ = x.shape
    n = b * s_new
    keep = GDN_CONV_WIDTH - 1
    assert s_new >= keep and s_new <= SUBLANES
    tm = _pick_tile(n, SAMPLE_ROWS_PER_STEP)
    q_bf, k, k_bf, v, v_bf, gqkv, gz, gates, s, st = _in_proj(x.reshape(n, D_MODEL), w, tm, FOX_HEAD_DIM ** -0.5)

    n_pool, page = k_pool.shape[:2]
    q4 = q_bf.reshape(b, s_new, FOX_HEADS, 1, FOX_HEAD_DIM)
    eye = jnp.eye(FOX_HEADS, dtype=BF16).reshape(1, 1, FOX_HEADS, FOX_HEADS, 1)
    q_bd = (q4 * eye).reshape(b, s_new * FOX_HEADS, FOX_WIDTH)
    pad_keys = lambda a: jnp.pad(a.reshape(b, s_new, FOX_WIDTH).transpose(0, 2, 1),
                                 ((0, 0), (0, 0), (0, LANES - s_new)))
    lf_new_t = jnp.pad(st[ROW_LOGF:ROW_LOGF + FOX_HEADS].reshape(FOX_HEADS, b, s_new).transpose(1, 0, 2),
                       ((0, 0), (0, 0), (0, LANES - s_new)))
    pool_t = lambda a: a.transpose(0, 2, 3, 1).reshape(n_pool, FOX_WIDTH, page)
    o_a = _fox_sample(page_table, q_bd, pool_t(k_pool), pool_t(v_pool), lf_pool.transpose(0, 2, 1),
                      pad_keys(k_bf), pad_keys(v_bf), lf_new_t, pages_per_step=_pick_tile(page_table.shape[1], PAGES_PER_STEP))

    chunk = SUBLANES
    pad_tok = lambda a: jnp.pad(a.reshape(b, s_new, -1), ((0, 0), (0, chunk - s_new), (0, 0)))
    xin = jnp.concatenate([jnp.pad(conv_buf, ((0, 0), (SUBLANES - keep, 0), (0, 0))), pad_tok(gqkv)], axis=1)
    seq_per_step = _pick_tile(b, LANES // chunk)
    o_b, ssm = _gdn_sample(xin.reshape(b * (SUBLANES + chunk), GDN_CONV_CH), pad_tok(s).reshape(b * chunk, LANES),
                           pad_tok(gz).reshape(b * chunk, GDN_V_WIDTH), w["conv_w"], ssm_state, w["gnorm"],
                           b, seq_per_step, chunk)
    o_b = o_b.reshape(b, chunk, GDN_V_WIDTH)[:, :s_new].reshape(n, GDN_V_WIDTH)

    y = _post(x.reshape(n, D_MODEL), o_a.reshape(n, FOX_WIDTH), o_b, gates, ple.reshape(n, PLE_DIM), w, tm)
    states = (k.reshape(b, s_new, FOX_HEADS, FOX_HEAD_DIM), v.reshape(b, s_new, FOX_HEADS, FOX_HEAD_DIM),
              s[:, ROW_LOGF:ROW_LOGF + FOX_HEADS].reshape(b, s_new, FOX_HEADS),
              gqkv.reshape(b, s_new, GDN_CONV_CH)[:, s_new - keep:], ssm)
    return y.reshape(b, s_new, D_MODEL), states


def kernel(x_prompt, x_sample, p_prompt, p_sample, cache_k, cache_v, cache_logf, state_conv, state_ssm, page_table,
           norm_mix_g, w_in, fox_f_bias, fox_q_norm_g, fox_k_norm_g, gdn_conv_w, gdn_a_log, gdn_dt_bias,
           gdn_out_norm_g, w_branch_a, w_branch_b, w_out, norm_mlp_g, w_up, w_down, norm_ple_g, w_ple_gate, w_ple):
    depth = w_in.shape[0]
    y_prompt, y_sample = x_prompt, x_sample
    prompt_states, sample_states = [], []
    for l in range(depth):
        w = _prepare_weights(l, norm_mix_g, w_in, fox_f_bias, fox_q_norm_g, fox_k_norm_g, gdn_conv_w, gdn_a_log,
                             gdn_dt_bias, gdn_out_norm_g, w_branch_a, w_branch_b, w_out, norm_mlp_g, w_up, w_down,
                             norm_ple_g, w_ple_gate, w_ple)
        y_prompt, st_p = _prompt_layer(y_prompt, p_prompt[l], w)
        y_sample, st_s = _sample_layer(y_sample, p_sample[l], w, cache_k[l], cache_v[l], cache_logf[l],
                                       state_conv[l], state_ssm[l], page_table)
        prompt_states.append(st_p)
        sample_states.append(st_s)
    stack = lambda states, i: jnp.stack([st[i] for st in states])
    return ((y_prompt, y_sample) + tuple(stack(prompt_states, i) for i in range(5))
            + tuple(stack(sample_states, i) for i in range(5)))
```

```python
import functools
import math

import jax
import jax.numpy as jnp
from jax import lax
from jax.experimental import pallas as pl
from jax.experimental.pallas import tpu as pltpu

F32 = jnp.float32
BF16 = jnp.bfloat16

D_MODEL = 1024
FOX_HEADS = 8
FOX_HEAD_DIM = 64
FOX_WIDTH = FOX_HEADS * FOX_HEAD_DIM
GDN_HEADS = 4
GDN_KEY_DIM = 128
GDN_VAL_DIM = 128
GDN_QK_WIDTH = GDN_HEADS * GDN_KEY_DIM
GDN_V_WIDTH = GDN_HEADS * GDN_VAL_DIM
GDN_CONV_WIDTH = 4
GDN_CONV_CH = 2 * GDN_QK_WIDTH + GDN_V_WIDTH
GDN_CHUNK = 64
D_FF = 4 * D_MODEL
PLE_DIM = 256
NORM_EPS = 1e-6

LANES = 128
SUBLANES = 8
NEG_BIG = -1e30
LOG2E = 1.4426950408889634
ZERO_PROB_LOG2 = 152.0
NORM_SLACK = 1.0 + 2.0 ** -7
VMEM_LIMIT = 56 * 1024 * 1024

_MAIN_SEGS = (FOX_WIDTH, FOX_WIDTH, FOX_WIDTH, GDN_CONV_CH, GDN_V_WIDTH, 2 * D_MODEL)
_MAIN_OFFS = tuple(sum(_MAIN_SEGS[:i]) for i in range(len(_MAIN_SEGS) + 1))
D_MAIN = _MAIN_OFFS[-1]
ROW_LOGF = 0
ROW_G = FOX_HEADS
ROW_BETA = FOX_HEADS + GDN_HEADS
N_SMALL = FOX_HEADS + 2 * GDN_HEADS


def _dot(a, b):
    return jnp.dot(a, b, preferred_element_type=F32)


def _dot_nt(a, b):
    return lax.dot_general(a, b, (((1,), (1,)), ((), ())), preferred_element_type=F32)


def _dot_tn(a, b):
    return lax.dot_general(a, b, (((0,), (0,)), ((), ())), preferred_element_type=F32)


def _split(a, parts, axis):
    pieces = []
    for _ in range(parts - 1):
        p = a.astype(BF16).astype(F32)
        pieces.append(p)
        a = a - p
    pieces.append(a)
    return jnp.concatenate(pieces, axis=axis).astype(BF16)


def _dot3(a, b):
    m, n = a.shape[0], b.shape[1]
    r = _dot(_split(a, 2, 0), _split(b, 2, 1))
    return (r[:m, :n] + r[:m, n:]) + (r[m:, :n] + r[m:, n:])


def _dot_exact_lhs(a_bf, b):
    n = b.shape[1]
    r = _dot(a_bf, _split(b, 3, 1))
    return r[:, :n] + (r[:, n:2 * n] + r[:, 2 * n:])


def _dot_exact_rhs(a, b_bf):
    m = a.shape[0]
    r = _dot(_split(a, 3, 0), b_bf)
    return r[:m] + (r[m:2 * m] + r[2 * m:])


def _sigmoid(x):
    return 0.5 * jnp.tanh(0.5 * x) + 0.5


def _iota(shape, dim):
    return lax.broadcasted_iota(jnp.int32, shape, dim)


def _div_pow2(x, divisor):
    shift = divisor.bit_length() - 1
    assert divisor == 1 << shift
    return lax.shift_right_logical(x, shift)


def _mod_pow2(x, divisor):
    assert divisor & (divisor - 1) == 0
    return x & (divisor - 1)


def _in_proj_kernel(x_ref, gmix_ref, wmain_ref, wsmall_ref, gmat_ref, qg_ref, kg_ref, sbias_ref, alog_ref,
                    q_ref, k_ref, kb_ref, v_ref, vb_ref, gqkv_ref, gz_ref, gates_ref, s_ref, st_ref,
                    *, feature_major_kv, q_scale):
    x = x_ref[...]
    xn = x * lax.rsqrt(jnp.mean(x * x, axis=-1, keepdims=True) + NORM_EPS) * gmix_ref[...]
    xb = xn.astype(BF16)

    def proj(seg):
        return _dot(xb, wmain_ref[:, _MAIN_OFFS[seg]:_MAIN_OFFS[seg + 1]])

    z = _dot(xb, wsmall_ref[...]).T + sbias_ref[...]
    row = _iota(z.shape, 0)
    t = jnp.log1p(jnp.exp(-jnp.abs(z)))
    logf = jnp.minimum(z, 0.0) - t
    g = -jnp.exp(alog_ref[...]) * (jnp.maximum(z, 0.0) + t)
    beta = _sigmoid(z)
    res = jnp.where(row < ROW_G, logf, jnp.where(row < ROW_BETA, g, jnp.where(row < N_SMALL, beta, 0.0)))
    st_ref[...] = res[:N_SMALL]
    s_ref[...] = res.T

    gmat = gmat_ref[...]

    def head_norm(t, g):
        ms = _dot((t * t).astype(BF16), gmat)
        return t * lax.rsqrt(ms + NORM_EPS) * g

    q = head_norm(proj(0), qg_ref[...])
    q_ref[...] = (q * q_scale).astype(BF16)
    k = head_norm(proj(1), kg_ref[...])
    kb_ref[...] = k.astype(BF16)
    v = proj(2)
    vb_ref[...] = v.astype(BF16)
    if feature_major_kv:
        k_ref[0] = k.T
        v_ref[0] = v.T
    else:
        k_ref[...] = k
        v_ref[...] = v
    gqkv_ref[...] = proj(3)
    gz_ref[...] = proj(4)
    gates_ref[...] = proj(5).astype(gates_ref.dtype)


def _in_proj(x, w, tm, q_scale, kv_seq_len=None):
    n = x.shape[0]
    const = lambda i: (0, 0)
    rows = lambda width: pl.BlockSpec((tm, width), lambda i: (i, 0))
    resident = lambda shape: pl.BlockSpec(shape, const, pipeline_mode=pl.Buffered(1))
    if kv_seq_len is None:
        kv_shape, kv_spec = jax.ShapeDtypeStruct((n, FOX_WIDTH), F32), rows(FOX_WIDTH)
    else:
        tiles = kv_seq_len // tm
        kv_shape = jax.ShapeDtypeStruct((n // kv_seq_len, FOX_WIDTH, kv_seq_len), F32)
        kv_spec = pl.BlockSpec((1, FOX_WIDTH, tm), lambda i: (i // tiles, 0, i % tiles))
    out_shape = (
        jax.ShapeDtypeStruct((n, FOX_WIDTH), BF16),
        kv_shape,
        jax.ShapeDtypeStruct((n, FOX_WIDTH), BF16),
        kv_shape,
        jax.ShapeDtypeStruct((n, FOX_WIDTH), BF16),
        jax.ShapeDtypeStruct((n, GDN_CONV_CH), F32),
        jax.ShapeDtypeStruct((n, GDN_V_WIDTH), F32),
        jax.ShapeDtypeStruct((n, 2 * D_MODEL), BF16),
        jax.ShapeDtypeStruct((n, LANES), F32),
        jax.ShapeDtypeStruct((N_SMALL, n), F32),
    )
    out_specs = (
        rows(FOX_WIDTH), kv_spec, rows(FOX_WIDTH), kv_spec, rows(FOX_WIDTH),
        rows(GDN_CONV_CH), rows(GDN_V_WIDTH), rows(2 * D_MODEL), rows(LANES),
        pl.BlockSpec((N_SMALL, tm), lambda i: (0, i)),
    )
    in_specs = [
        rows(D_MODEL),
        resident((1, D_MODEL)),
        resident((D_MODEL, D_MAIN)),
        resident((D_MODEL, LANES)),
        resident((FOX_WIDTH, FOX_WIDTH)),
        resident((1, FOX_WIDTH)),
        resident((1, FOX_WIDTH)),
        resident((LANES, 1)),
        resident((LANES, 1)),
    ]
    return pl.pallas_call(
        functools.partial(_in_proj_kernel, feature_major_kv=kv_seq_len is not None, q_scale=q_scale),
        grid=(n // tm,),
        in_specs=in_specs,
        out_specs=out_specs,
        out_shape=out_shape,
        compiler_params=pltpu.CompilerParams(dimension_semantics=("arbitrary",), vmem_limit_bytes=VMEM_LIMIT),
        name="in_proj",
    )(x, w["gmix"], w["w_main"], w["w_small"], w["gmat"], w["qg"], w["kg"], w["sbias"], w["alog"])


def _cumsum_kernel(lf_ref, cum_ref, *, scale):
    n_chunks = lf_ref.shape[1] // LANES
    upper = (_iota((LANES, LANES), 0) <= _iota((LANES, LANES), 1)).astype(BF16)
    chunks = jnp.concatenate([lf_ref[:, c * LANES:(c + 1) * LANES] for c in range(n_chunks)], axis=0)
    local = _dot_exact_rhs(chunks, upper)
    carry = jnp.zeros((FOX_HEADS, 1), F32)
    for c in range(n_chunks):
        cs = carry + local[c * FOX_HEADS:(c + 1) * FOX_HEADS]
        cum_ref[:, c * LANES:(c + 1) * LANES] = cs * scale
        carry = cs[:, LANES - 1:LANES]


def _seq_cumsum(st, seq_len, scale):
    n = st.shape[1]
    spec = pl.BlockSpec((FOX_HEADS, seq_len), lambda b: (0, b))
    return pl.pallas_call(
        functools.partial(_cumsum_kernel, scale=scale),
        grid=(n // seq_len,),
        in_specs=[spec],
        out_specs=spec,
        out_shape=jax.ShapeDtypeStruct((FOX_HEADS, n), F32),
        compiler_params=pltpu.CompilerParams(dimension_semantics=("arbitrary",)),
        name="fox_cumsum",
    )(st)


def _fox_prompt_kernel(q_ref, k_ref, v_ref, lf_ref, o_ref, kmax_sc, ckmin_sc, ck_ref, *, tq):
    qi = pl.program_id(2)
    lane = _iota((tq, LANES), 1)
    first = lane < FOX_HEAD_DIM
    in_head = (first, jnp.logical_not(first))
    tile_lane = _iota((1, LANES), 1)
    heads = (0, 1)

    @pl.when(qi == 0)
    def _():
        n_chunks = lf_ref.shape[3] // LANES
        upper = (_iota((LANES, LANES), 0) <= _iota((LANES, LANES), 1)).astype(BF16)
        chunks = jnp.concatenate([lf_ref[0, h, :, c * LANES:(c + 1) * LANES] for h in heads for c in range(n_chunks)],
                                 axis=0)
        local = _dot_exact_rhs(chunks, upper)
        for h in heads:
            carry = jnp.zeros((1, 1), F32)
            for c in range(n_chunks):
                cs = carry + local[h * n_chunks + c:h * n_chunks + c + 1]
                ck_ref[0, h, :, c * LANES:(c + 1) * LANES] = cs * LOG2E
                carry = cs[:, LANES - 1:LANES]

        def scan(c, carry):
            sq_max, ck_min = carry
            sl = pl.ds(pl.multiple_of(c * tq, tq), tq)
            kk = k_ref[0, sl, :].astype(F32)
            sq = kk * kk
            sq_max = tuple(jnp.maximum(sq_max[h], jnp.max(jnp.sum(jnp.where(in_head[h], sq, 0.0), axis=-1,
                                                                  keepdims=True), axis=0, keepdims=True))
                           for h in heads)
            ck_min = tuple(jnp.where(tile_lane == c, jnp.min(ck_ref[0, h, :, sl], axis=-1, keepdims=True), ck_min[h])
                           for h in heads)
            return sq_max, ck_min
        zero, zero_row = jnp.zeros((1, 1), F32), jnp.zeros((1, LANES), F32)
        sq_max, ck_min = lax.fori_loop(0, k_ref.shape[1] // tq, scan, ((zero, zero), (zero_row, zero_row)))
        for h in heads:
            kmax_sc[h] = jnp.broadcast_to(jnp.sqrt(sq_max[h]), kmax_sc.shape[1:])
            ckmin_sc[h] = jnp.broadcast_to(ck_min[h], ckmin_sc.shape[1:])

    q2 = q_ref[0]
    causal = _iota((tq, tq), 0) >= _iota((tq, tq), 1)
    qs, cqs, reach = [], [], []
    for h in heads:
        qh = jnp.where(in_head[h], q2, jnp.zeros_like(q2))
        cq_row = ck_ref[0, h, :, pl.ds(pl.multiple_of(qi * tq, tq), tq)]
        cq = jnp.broadcast_to(cq_row, (LANES, tq)).T[:, 0:1]
        qf = qh.astype(F32)
        q_norm = jnp.sqrt(jnp.sum(qf * qf, axis=-1, keepdims=True))
        qs.append(qh)
        cqs.append(cq)
        reach.append(q_norm * kmax_sc[h][0:1, 0:1] * NORM_SLACK + cq)

    def steps(hs, j, carries, diagonal):
        sl = pl.ds(pl.multiple_of(j * tq, tq), tq)
        k2 = k_ref[0, sl, :]
        v2 = v_ref[0, sl, :]
        ss = [(cqs[h] - ck_ref[0, h, :, sl]) + _dot_nt(qs[h], k2) for h in hs]
        if diagonal:
            ss = [jnp.where(causal, s, -jnp.inf) for s in ss]
        m_news = [jnp.maximum(c[0], jnp.max(s, axis=-1, keepdims=True)) for c, s in zip(carries, ss)]
        alphas = [jnp.exp2(c[0] - m_new) for c, m_new in zip(carries, m_news)]
        ps = [jnp.exp2(s - m_new) for s, m_new in zip(ss, m_news)]
        ls = [alpha * c[1] + jnp.sum(p, axis=-1, keepdims=True) for alpha, c, p in zip(alphas, carries, ps)]
        accs = [alpha * c[2] + _dot(p.astype(BF16), v2) for alpha, c, p in zip(alphas, carries, ps)]
        return tuple(zip(m_news, ls, accs))

    def tiles_reached(h, m):
        gap = jnp.max(reach[h] - m, axis=0, keepdims=True) - ckmin_sc[h][0:1, :]
        hit = jnp.logical_and(tile_lane < qi, gap >= -ZERO_PROB_LOG2)
        return jnp.sum(hit.astype(F32)).astype(jnp.int32)

    col = lambda val: jnp.full((tq, 1), val, F32)
    init = (col(NEG_BIG), col(0.0), jnp.zeros((tq, LANES), F32))
    ca, cb = steps(heads, qi, (init, init), diagonal=True)
    n_a, n_b = tiles_reached(0, ca[0]), tiles_reached(1, cb[0])
    n_both = jnp.minimum(n_a, n_b)
    ca, cb = lax.fori_loop(0, n_both, lambda i, c: steps(heads, qi - 1 - i, c, diagonal=False), (ca, cb))
    ca = lax.fori_loop(n_both, n_a, lambda i, c: steps((0,), qi - 1 - i, (c,), diagonal=False)[0], ca)
    cb = lax.fori_loop(n_both, n_b, lambda i, c: steps((1,), qi - 1 - i, (c,), diagonal=False)[0], cb)
    o_ref[0] = jnp.where(first, ca[2] * (1.0 / ca[1]), cb[2] * (1.0 / cb[1])).astype(o_ref.dtype)


def _fox_prompt(qb, kb, vb, ck, tq):
    b, seq_len, _ = qb.shape
    pairs = FOX_HEADS // 2
    return pl.pallas_call(
        functools.partial(_fox_prompt_kernel, tq=tq),
        grid=(b, pairs, seq_len // tq),
        in_specs=[
            pl.BlockSpec((1, tq, LANES), lambda b_, p, i: (b_, i, p)),
            pl.BlockSpec((1, seq_len, LANES), lambda b_, p, i: (b_, 0, p)),
            pl.BlockSpec((1, seq_len, LANES), lambda b_, p, i: (b_, 0, p)),
            pl.BlockSpec((1, 2, 1, seq_len), lambda b_, p, i: (b_, p, 0, 0)),
        ],
        out_specs=pl.BlockSpec((1, tq, LANES), lambda b_, p, i: (b_, i, p)),
        out_shape=jax.ShapeDtypeStruct((b, seq_len, FOX_WIDTH), BF16),
        scratch_shapes=[pltpu.VMEM((2, SUBLANES, LANES), F32), pltpu.VMEM((2, SUBLANES, LANES), F32),
                        pltpu.VMEM((1, 2, 1, seq_len), F32)],
        compiler_params=pltpu.CompilerParams(dimension_semantics=("arbitrary", "arbitrary", "arbitrary"),
                                             vmem_limit_bytes=VMEM_LIMIT),
        name="fox_prompt_attention",
    )(qb, kb, vb, ck)


def _fox_sample_kernel(pt_ref, q_ref, *refs, pages_per_step, n_new):
    del pt_ref
    pp = pages_per_step
    k_refs, v_refs, lf_refs = refs[:pp], refs[pp:2 * pp], refs[2 * pp:3 * pp]
    knew_ref, vnew_ref, lfnew_ref, o_ref, m_sc, l_sc, acc_sc = refs[3 * pp:]
    step_id = pl.program_id(1)
    n_rows = n_new * FOX_HEADS

    @pl.when(step_id == 0)
    def _():
        m_sc[...] = jnp.full(m_sc.shape, NEG_BIG, F32)
        l_sc[...] = jnp.zeros(l_sc.shape, F32)
        acc_sc[...] = jnp.zeros(acc_sc.shape, F32)

    q = q_ref[0]
    ri = _iota((LANES, LANES), 0)
    ci = _iota((LANES, LANES), 1)
    after = (ri > ci).astype(BF16)
    upto = (ri <= ci).astype(BF16)
    tile_heads = lambda a: jnp.concatenate([a] * n_new, axis=0)

    def update(s, shift, v_t):
        m_prev = m_sc[...] + shift
        m_new = jnp.maximum(m_prev, jnp.max(s, axis=-1, keepdims=True))
        alpha = jnp.exp(m_prev - m_new)
        p = jnp.exp(s - m_new)
        l_sc[...] = alpha * l_sc[...] + jnp.sum(p, axis=-1, keepdims=True)
        acc_sc[...] = alpha * acc_sc[...] + _dot_nt(p.astype(BF16), v_t)
        m_sc[...] = m_new

    lf = jnp.concatenate([r[0] for r in lf_refs], axis=0)
    suffix = _dot_exact_rhs(lf, after)
    total = suffix[:, 0:1] + lf[:, 0:1]
    bias, later = [], jnp.zeros((FOX_HEADS, 1), F32)
    for j in reversed(range(pp)):
        rows = slice(j * FOX_HEADS, (j + 1) * FOX_HEADS)
        bias.append(tile_heads(suffix[rows] + later))
        later = later + total[rows]
    bias = jnp.concatenate(bias[::-1], axis=1)
    k_t = jnp.concatenate([r[0].astype(BF16) for r in k_refs], axis=1)
    v_t = jnp.concatenate([r[0].astype(BF16) for r in v_refs], axis=1)
    update(bias + _dot(q, k_t), tile_heads(later), v_t)

    @pl.when(step_id == pl.num_programs(1) - 1)
    def _():
        cum_rows = tile_heads(_dot_exact_rhs(lfnew_ref[0], upto))
        lane = _iota((n_rows, LANES), 1)
        query = _div_pow2(_iota((n_rows, LANES), 0), FOX_HEADS)
        cum_q = jnp.sum(jnp.where(lane == query, cum_rows, 0.0), axis=-1, keepdims=True)
        s = (cum_q - cum_rows) + _dot(q, knew_ref[0])
        s = jnp.where(lane <= query, s, -jnp.inf)
        update(s, cum_q, vnew_ref[0])
        out = acc_sc[...] / l_sc[...]
        own = _div_pow2(_iota(out.shape, 1), FOX_HEAD_DIM) == _mod_pow2(_iota(out.shape, 0), FOX_HEADS)
        out = jnp.where(own, out, 0.0)
        o_ref[0] = jnp.concatenate(
            [jnp.sum(out[i * FOX_HEADS:(i + 1) * FOX_HEADS], axis=0, keepdims=True) for i in range(n_new)], axis=0)


def _fox_sample(page_table, q_bd, k_pool_t, v_pool_t, lf_pool_t, k_new_t, v_new_t, lf_new_t, pages_per_step):
    n_seq, n_pages = page_table.shape
    n_rows = q_bd.shape[1]
    n_new = n_rows // FOX_HEADS
    page = k_pool_t.shape[2]
    pp = pages_per_step
    steps = n_pages // pp

    def paged(shape, j):
        return pl.BlockSpec(shape, lambda b, s, pt: (pt[b * n_pages + s * pp + j], 0, 0))

    per_seq = lambda shape: pl.BlockSpec(shape, lambda b, s, pt: (b, 0, 0))
    in_specs = ([per_seq((1, n_rows, FOX_WIDTH))]
                + [paged((1, FOX_WIDTH, page), j) for j in range(pp)]
                + [paged((1, FOX_WIDTH, page), j) for j in range(pp)]
                + [paged((1, FOX_HEADS, page), j) for j in range(pp)]
                + [per_seq((1, FOX_WIDTH, LANES)), per_seq((1, FOX_WIDTH, LANES)), per_seq((1, FOX_HEADS, LANES))])
    grid_spec = pltpu.PrefetchScalarGridSpec(
        num_scalar_prefetch=1,
        grid=(n_seq, steps),
        in_specs=in_specs,
        out_specs=per_seq((1, n_new, FOX_WIDTH)),
        scratch_shapes=[pltpu.VMEM((n_rows, 1), F32), pltpu.VMEM((n_rows, 1), F32),
                        pltpu.VMEM((n_rows, FOX_WIDTH), F32)],
    )
    return pl.pallas_call(
        functools.partial(_fox_sample_kernel, pages_per_step=pp, n_new=n_new),
        grid_spec=grid_spec,
        out_shape=jax.ShapeDtypeStruct((n_seq, n_new, FOX_WIDTH), F32),
        compiler_params=pltpu.CompilerParams(dimension_semantics=("arbitrary", "arbitrary"),
                                             vmem_limit_bytes=VMEM_LIMIT),
        name="fox_sample_attention",
    )(page_table.reshape(-1), q_bd, *([k_pool_t] * pp), *([v_pool_t] * pp), *([lf_pool_t] * pp),
      k_new_t, v_new_t, lf_new_t)


def _dot1(a, b):
    return _dot(a.astype(BF16), b.astype(BF16))


def _unit_lower_inverses(mats, same_block, chunk, base, dot):
    t_len = mats[0].shape[0]
    eye = (_iota((t_len, t_len), 0) == _iota((t_len, t_len), 1)).astype(F32)
    size = min(base, chunk)
    in_base = same_block(size)
    ns = [jnp.where(in_base, -a, 0.0) for a in mats]
    invs = [eye + n for n in ns]
    power = 1
    while 2 * power < size:
        ns = [dot(n, n) for n in ns]
        invs = [inv + dot(inv, n) for inv, n in zip(invs, ns)]
        power *= 2
    while size < chunk:
        off_mask = same_block(2 * size) & jnp.logical_not(same_block(size))
        corr = [dot(jnp.where(off_mask, a, 0.0), inv) for a, inv in zip(mats, invs)]
        invs = [inv - dot(inv, c) for inv, c in zip(invs, corr)]
        size *= 2
    return invs


def _short_conv(xbuf, conv_w, n_rows):
    first = SUBLANES - GDN_CONV_WIDTH + 1
    y = conv_w[0:1] * xbuf[first:first + n_rows, :]
    for i in range(1, GDN_CONV_WIDTH):
        y = y + conv_w[i:i + 1] * xbuf[first + i:first + i + n_rows, :]
    return y * _sigmoid(y)


class _ChunkMasks:
    def __init__(self, t_len, chunk):
        self.t_len, self.chunk = t_len, chunk
        ri = _iota((t_len, t_len), 0)
        ci = _iota((t_len, t_len), 1)
        self.same_block = lambda size: _div_pow2(ri, size) == _div_pow2(ci, size)
        in_chunk = self.same_block(chunk)
        self.lower = in_chunk & (ri >= ci)
        self.strict = in_chunk & (ri > ci)


def _chunk_cumsums(sblk, masks):
    chunk = masks.chunk
    cum_all = _dot_exact_lhs(masks.lower.astype(BF16), sblk)
    last_all = jnp.concatenate(
        [jnp.broadcast_to(cum_all[(c + 1) * chunk - 1:(c + 1) * chunk, :], (chunk, LANES))
         for c in range(masks.t_len // chunk)], axis=0)
    return cum_all, cum_all.T, last_all


def _delta_rule_chunk_terms(ybuf, sblks, cums, masks):
    sub = masks.t_len
    units = [(i, h) for i in range(len(sblks)) for h in range(GDN_HEADS)]
    col = lambda a, base, h: a[:, base + h:base + h + 1]
    betas = [col(sblks[i], ROW_BETA, h) for i, h in units]
    cumcols = [col(cums[i][0], ROW_G, h) for i, h in units]
    lasts = [col(cums[i][2], ROW_G, h) for i, h in units]
    decays = [jnp.where(masks.lower,
                        jnp.exp(jnp.where(masks.lower, cumcols[u] - cums[i][1][ROW_G + h:ROW_G + h + 1, :], 0.0)), 0.0)
              for u, (i, h) in enumerate(units)]
    qs, ks, vs = [], [], []
    for i, h in units:
        rows = slice(i * sub, (i + 1) * sub)
        q = ybuf[rows, h * GDN_KEY_DIM:(h + 1) * GDN_KEY_DIM]
        k = ybuf[rows, GDN_QK_WIDTH + h * GDN_KEY_DIM:GDN_QK_WIDTH + (h + 1) * GDN_KEY_DIM]
        qs.append(q * lax.rsqrt(jnp.sum(q * q, axis=-1, keepdims=True) + NORM_EPS) * (GDN_KEY_DIM ** -0.5))
        ks.append(k * lax.rsqrt(jnp.sum(k * k, axis=-1, keepdims=True) + NORM_EPS))
        vs.append(ybuf[rows, 2 * GDN_QK_WIDTH + h * GDN_VAL_DIM:2 * GDN_QK_WIDTH + (h + 1) * GDN_VAL_DIM])
    n = range(len(units))
    k_betas = [ks[u] * betas[u] for u in n]
    e_cums = [jnp.exp(cumcols[u]) for u in n]
    k_bfs = [ks[u].astype(BF16) for u in n]
    amats = [jnp.where(masks.strict, _dot_nt(k_betas[u].astype(BF16), k_bfs[u]) * decays[u], 0.0) for u in n]
    tmats = _unit_lower_inverses(amats, masks.same_block, masks.chunk, base=SUBLANES, dot=_dot1)
    uws = [_dot(tmats[u].astype(BF16),
                jnp.concatenate([vs[u] * betas[u], k_betas[u] * e_cums[u]], axis=-1).astype(BF16)) for u in n]
    qks = [(_dot_nt(qs[u].astype(BF16), k_bfs[u]) * decays[u]).astype(BF16) for u in n]
    flat = [(uws[u][:, :GDN_VAL_DIM], uws[u][:, GDN_VAL_DIM:].astype(BF16), qks[u], (qs[u] * e_cums[u]).astype(BF16),
             (ks[u] * jnp.exp(lasts[u] - cumcols[u])).astype(BF16), jnp.exp(lasts[u])) for u in n]
    return [flat[i * GDN_HEADS:(i + 1) * GDN_HEADS] for i in range(len(sblks))]


def _gated_out_norm(o, gz, gnorm):
    o = o * lax.rsqrt(jnp.mean(o * o, axis=-1, keepdims=True) + NORM_EPS) * gnorm
    return o * (gz * _sigmoid(gz))


def _gdn_prompt_kernel(gqkv_ref, s_ref, gz_ref, convw_ref, conv0_ref, s0_ref, gnorm_ref, o_ref, sfin_ref,
                       xbuf, ybuf, state, *, t_len, chunk):
    t = pl.program_id(1)

    @pl.when(t == 0)
    def _():
        xbuf[0:SUBLANES, :] = conv0_ref[0]
        state[...] = s0_ref[0]

    xbuf[SUBLANES:SUBLANES + t_len, :] = gqkv_ref[...]
    ybuf[...] = _short_conv(xbuf, convw_ref[...], t_len)
    xbuf[0:SUBLANES, :] = xbuf[t_len:t_len + SUBLANES, :]

    sub = min(t_len, LANES)
    masks = _ChunkMasks(sub, chunk)
    sblks = [s_ref[i * sub:(i + 1) * sub, :] for i in range(t_len // sub)]
    terms = _delta_rule_chunk_terms(ybuf, sblks, [_chunk_cumsums(sb, masks) for sb in sblks], masks)
    outs = [[] for _ in range(GDN_HEADS)]
    for sub_terms in terms:
        for c in range(sub // chunk):
            rows = slice(c * chunk, (c + 1) * chunk)
            for h, (u, wmat, qk, q_dec, k_dec, chunk_decay) in enumerate(sub_terms):
                st = state[h]
                ws = _dot(jnp.concatenate([wmat[rows], q_dec[rows]], axis=0), st.astype(BF16))
                v_new_bf = (u[rows] - ws[:chunk]).astype(BF16)
                outs[h].append(ws[chunk:] + _dot(qk[rows, rows], v_new_bf))
                state[h] = st * chunk_decay[c * chunk:c * chunk + 1] + _dot_tn(k_dec[rows], v_new_bf)
    for h in range(GDN_HEADS):
        head_cols = slice(h * GDN_VAL_DIM, (h + 1) * GDN_VAL_DIM)
        o_ref[:, head_cols] = _gated_out_norm(jnp.concatenate(outs[h], axis=0), gz_ref[:, head_cols],
                                              gnorm_ref[...]).astype(o_ref.dtype)

    @pl.when(t == pl.num_programs(1) - 1)
    def _():
        sfin_ref[0] = state[...]


def _gdn_prompt(gqkv, s, gz, conv_w, conv0, s0, gnorm, n_seq, t_len, chunk):
    n = gqkv.shape[0]
    tiles = n // n_seq // t_len
    rows = lambda width: pl.BlockSpec((t_len, width), lambda b, t: (b * tiles + t, 0))
    const = lambda shape: pl.BlockSpec(shape, lambda b, t: (0,) * len(shape))
    state_spec = pl.BlockSpec((1, GDN_HEADS, GDN_KEY_DIM, GDN_VAL_DIM), lambda b, t: (b, 0, 0, 0))
    return pl.pallas_call(
        functools.partial(_gdn_prompt_kernel, t_len=t_len, chunk=chunk),
        grid=(n_seq, tiles),
        in_specs=[rows(GDN_CONV_CH), rows(LANES), rows(GDN_V_WIDTH), const((GDN_CONV_WIDTH, GDN_CONV_CH)),
                  pl.BlockSpec((1, SUBLANES, GDN_CONV_CH), lambda b, t: (b, 0, 0)), state_spec,
                  const((1, GDN_VAL_DIM))],
        out_specs=(rows(GDN_V_WIDTH), state_spec),
        out_shape=(jax.ShapeDtypeStruct((n, GDN_V_WIDTH), BF16),
                   jax.ShapeDtypeStruct((n_seq, GDN_HEADS, GDN_KEY_DIM, GDN_VAL_DIM), F32)),
        scratch_shapes=[pltpu.VMEM((t_len + SUBLANES, GDN_CONV_CH), F32), pltpu.VMEM((t_len, GDN_CONV_CH), F32),
                        pltpu.VMEM((GDN_HEADS, GDN_KEY_DIM, GDN_VAL_DIM), F32)],
        compiler_params=pltpu.CompilerParams(dimension_semantics=("arbitrary", "arbitrary"),
                                             vmem_limit_bytes=VMEM_LIMIT),
        name="gated_deltanet_prompt",
    )(gqkv, s, gz, conv_w, conv0, s0, gnorm)


def _gdn_sample_kernel(xin_ref, s_ref, gz_ref, convw_ref, s0_ref, gnorm_ref, o_ref, sfin_ref, xbuf, ybuf,
                       *, n_seq, chunk):
    per_seq = SUBLANES + chunk
    n_in = n_seq * per_seq
    t_len = n_seq * chunk
    xbuf[0:SUBLANES, :] = jnp.zeros((SUBLANES, GDN_CONV_CH), F32)
    xbuf[SUBLANES:SUBLANES + n_in, :] = xin_ref[...]
    y = _short_conv(xbuf, convw_ref[...], n_in)
    for i in range(n_seq):
        ybuf[i * chunk:(i + 1) * chunk, :] = y[i * per_seq + SUBLANES:(i + 1) * per_seq]

    masks = _ChunkMasks(t_len, chunk)
    sblk = s_ref[...]
    cums = _chunk_cumsums(sblk, masks)
    seq_of_row = _div_pow2(_iota((t_len, GDN_KEY_DIM), 0), chunk)
    terms = _delta_rule_chunk_terms(ybuf, [sblk], [cums], masks)[0]
    for h, (u, wmat, qk, q_dec, k_dec, chunk_decay) in enumerate(terms):
        v_news, reads = [], []
        for i in range(n_seq):
            rows = slice(i * chunk, (i + 1) * chunk)
            ws = _dot(jnp.concatenate([wmat[rows], q_dec[rows]], axis=0), s0_ref[i, h].astype(BF16))
            v_news.append(u[rows] - ws[:chunk])
            reads.append(ws[chunk:])
        v_new_bf = jnp.concatenate(v_news, axis=0).astype(BF16)
        o = jnp.concatenate(reads, axis=0) + _dot(qk, v_new_bf)
        for i in range(n_seq):
            own_rows = jnp.where(seq_of_row == i, k_dec, jnp.zeros_like(k_dec))
            sfin_ref[i, h] = (s0_ref[i, h] * chunk_decay[i * chunk:i * chunk + 1]
                              + _dot_tn(own_rows, v_new_bf))
        head_cols = slice(h * GDN_VAL_DIM, (h + 1) * GDN_VAL_DIM)
        o_ref[:, head_cols] = _gated_out_norm(o, gz_ref[:, head_cols], gnorm_ref[...])


def _gdn_sample(xin, s, gz, conv_w, s0, gnorm, n_seq_total, seq_per_step, chunk):
    steps = n_seq_total // seq_per_step
    per_seq = SUBLANES + chunk
    t_len = seq_per_step * chunk
    rows = lambda width: pl.BlockSpec((t_len, width), lambda i: (i, 0))
    const = lambda shape: pl.BlockSpec(shape, lambda i: (0,) * len(shape))
    state_spec = pl.BlockSpec((seq_per_step, GDN_HEADS, GDN_KEY_DIM, GDN_VAL_DIM), lambda i: (i, 0, 0, 0))
    return pl.pallas_call(
        functools.partial(_gdn_sample_kernel, n_seq=seq_per_step, chunk=chunk),
        grid=(steps,),
        in_specs=[pl.BlockSpec((seq_per_step * per_seq, GDN_CONV_CH), lambda i: (i, 0)), rows(LANES),
                  rows(GDN_V_WIDTH), const((GDN_CONV_WIDTH, GDN_CONV_CH)), state_spec, const((1, GDN_VAL_DIM))],
        out_specs=(rows(GDN_V_WIDTH), state_spec),
        out_shape=(jax.ShapeDtypeStruct((n_seq_total * chunk, GDN_V_WIDTH), F32),
                   jax.ShapeDtypeStruct((n_seq_total, GDN_HEADS, GDN_KEY_DIM, GDN_VAL_DIM), F32)),
        scratch_shapes=[pltpu.VMEM((seq_per_step * per_seq + SUBLANES, GDN_CONV_CH), F32),
                        pltpu.VMEM((t_len, GDN_CONV_CH), F32)],
        compiler_params=pltpu.CompilerParams(dimension_semantics=("arbitrary",), vmem_limit_bytes=VMEM_LIMIT),
        name="gated_deltanet_sample",
    )(xin, s, gz, conv_w, s0, gnorm)


def _post_kernel(x_ref, oa_ref, ob_ref, gates_ref, ple_ref, wa_ref, wb_ref, wout_ref, gmlp_ref, wup_ref,
                 wdown_ref, gple_ref, wpg_ref, wple_ref, y_ref):
    def rms(a, g):
        return a * lax.rsqrt(jnp.mean(a * a, axis=-1, keepdims=True) + NORM_EPS) * g

    gates = _sigmoid(gates_ref[...].astype(F32))
    merged = (gates[:, :D_MODEL] * _dot(oa_ref[...].astype(BF16), wa_ref[...])
              + gates[:, D_MODEL:] * _dot(ob_ref[...].astype(BF16), wb_ref[...]))
    x = x_ref[...] + _dot(merged.astype(BF16), wout_ref[...])
    up = jnp.maximum(_dot(rms(x, gmlp_ref[...]).astype(BF16), wup_ref[...]), 0.0)
    x = x + _dot((up * up).astype(BF16), wdown_ref[...])
    ple_gate = _sigmoid(_dot(rms(x, gple_ref[...]).astype(BF16), wpg_ref[...]))
    y_ref[...] = x + ple_gate * _dot(ple_ref[...].astype(BF16), wple_ref[...])


def _post(x, oa, ob, gates, ple, w, tm):
    n = x.shape[0]
    rows = lambda width: pl.BlockSpec((tm, width), lambda i: (i, 0))
    resident = lambda shape: pl.BlockSpec(shape, lambda i: (0, 0), pipeline_mode=pl.Buffered(1))
    return pl.pallas_call(
        _post_kernel,
        grid=(n // tm,),
        in_specs=[rows(D_MODEL), rows(FOX_WIDTH), rows(GDN_V_WIDTH), rows(2 * D_MODEL), rows(PLE_DIM),
                  resident((FOX_WIDTH, D_MODEL)), resident((GDN_V_WIDTH, D_MODEL)), resident((D_MODEL, D_MODEL)),
                  resident((1, D_MODEL)), resident((D_MODEL, D_FF)), resident((D_FF, D_MODEL)),
                  resident((1, D_MODEL)), resident((D_MODEL, D_MODEL)), resident((PLE_DIM, D_MODEL))],
        out_specs=rows(D_MODEL),
        out_shape=jax.ShapeDtypeStruct((n, D_MODEL), F32),
        compiler_params=pltpu.CompilerParams(dimension_semantics=("arbitrary",), vmem_limit_bytes=VMEM_LIMIT),
        name="merge_mlp_ple",
    )(x, oa, ob, gates, ple, w["w_a"], w["w_b"], w["w_out"], w["gmlp"], w["w_up"], w["w_down"], w["gple"],
      w["w_pg"], w["w_ple"])


def _prepare_weights(l, norm_mix_g, w_in, fox_f_bias, fox_q_norm_g, fox_k_norm_g, gdn_conv_w, gdn_a_log, gdn_dt_bias,
                     gdn_out_norm_g, w_branch_a, w_branch_b, w_out, norm_mlp_g, w_up, w_down, norm_ple_g,
                     w_ple_gate, w_ple):
    wi = w_in[l]
    o_ff = 3 * FOX_WIDTH
    o_gqkv = o_ff + FOX_HEADS
    o_ga = o_gqkv + GDN_CONV_CH
    o_gb = o_ga + GDN_HEADS
    o_gz = o_gb + GDN_HEADS
    o_gates = o_gz + GDN_V_WIDTH
    w_main = jnp.concatenate([wi[:, :o_ff], wi[:, o_gqkv:o_ga], wi[:, o_gz:]], axis=1).astype(BF16)
    w_small = jnp.concatenate([wi[:, o_ff:o_gqkv], wi[:, o_ga:o_gz]], axis=1)
    w_small = jnp.pad(w_small, ((0, 0), (0, LANES - N_SMALL))).astype(BF16)
    pad_col = lambda parts: jnp.pad(jnp.concatenate(parts), (0, LANES - N_SMALL)).reshape(LANES, 1).astype(F32)
    zeros_h = jnp.zeros((GDN_HEADS,), F32)
    head = jnp.arange(FOX_WIDTH) // FOX_HEAD_DIM
    return dict(
        gmix=norm_mix_g[l].reshape(1, D_MODEL),
        w_main=w_main,
        w_small=w_small,
        gmat=((head[:, None] == head[None, :]).astype(F32) / FOX_HEAD_DIM).astype(BF16),
        qg=jnp.tile(fox_q_norm_g[l], FOX_HEADS).reshape(1, FOX_WIDTH),
        kg=jnp.tile(fox_k_norm_g[l], FOX_HEADS).reshape(1, FOX_WIDTH),
        sbias=pad_col([fox_f_bias[l], gdn_dt_bias[l], zeros_h]),
        alog=pad_col([jnp.zeros((FOX_HEADS,), F32), gdn_a_log[l], zeros_h]),
        conv_w=gdn_conv_w[l],
        gnorm=gdn_out_norm_g[l].reshape(1, GDN_VAL_DIM),
        w_a=w_branch_a[l].astype(BF16),
        w_b=w_branch_b[l].astype(BF16),
        w_out=w_out[l].astype(BF16),
        gmlp=norm_mlp_g[l].reshape(1, D_MODEL),
        w_up=w_up[l].astype(BF16),
        w_down=w_down[l].astype(BF16),
        gple=norm_ple_g[l].reshape(1, D_MODEL),
        w_pg=w_ple_gate[l].astype(BF16),
        w_ple=w_ple[l].astype(BF16),
    )


ROWS_PER_STEP = 512
SAMPLE_ROWS_PER_STEP = 256
ATTN_TILE = 1024
GDN_CHUNKS_PER_STEP = 8
PAGES_PER_STEP = 32


def _pick_tile(n, target):
    t = min(n, target)
    while n % t:
        t //= 2
    return t


def _prompt_layer(x, ple, w):
    b, seq_len, _ = x.shape
    n = b * seq_len
    tm = _pick_tile(n, ROWS_PER_STEP)
    q_bf, k_t, k_bf, v_t, v_bf, gqkv, gz, gates, s, st = _in_proj(
        x.reshape(n, D_MODEL), w, tm, FOX_HEAD_DIM ** -0.5 * LOG2E, kv_seq_len=seq_len)

    ck = st[ROW_LOGF:ROW_LOGF + FOX_HEADS].reshape(FOX_HEADS, b, 1, seq_len).transpose(1, 0, 2, 3)
    tq = _pick_tile(seq_len, ATTN_TILE)
    o_a = _fox_prompt(q_bf.reshape(b, seq_len, FOX_WIDTH), k_bf.reshape(b, seq_len, FOX_WIDTH),
                      v_bf.reshape(b, seq_len, FOX_WIDTH), ck, tq)

    chunk = math.gcd(seq_len, GDN_CHUNK)
    t_len = _pick_tile(seq_len, GDN_CHUNKS_PER_STEP * chunk)
    conv0 = jnp.zeros((b, SUBLANES, GDN_CONV_CH), F32)
    ssm0 = jnp.zeros((b, GDN_HEADS, GDN_KEY_DIM, GDN_VAL_DIM), F32)
    o_b, ssm = _gdn_prompt(gqkv, s, gz, w["conv_w"], conv0, ssm0, w["gnorm"], b, t_len, chunk)

    y = _post(x.reshape(n, D_MODEL), o_a.reshape(n, FOX_WIDTH), o_b, gates, ple.reshape(n, PLE_DIM), w,
              tm)
    keep = GDN_CONV_WIDTH - 1
    token_major = lambda a: a.reshape(b, FOX_HEADS, FOX_HEAD_DIM, seq_len).transpose(0, 3, 1, 2)
    states = (token_major(k_t), token_major(v_t),
              st[ROW_LOGF:ROW_LOGF + FOX_HEADS].reshape(FOX_HEADS, b, seq_len).transpose(1, 2, 0),
              gqkv.reshape(b, seq_len, GDN_CONV_CH)[:, seq_len - keep:], ssm)
    return y.reshape(b, seq_len, D_MODEL), states


def _sample_layer(x, ple, w, k_pool, v_pool, lf_pool, conv_buf, ssm_state, page_table):
    b, s_new, _ = x.shape
    n = b * s_new
    keep = GDN_CONV_WIDTH - 1
    assert s_new >= keep and s_new <= SUBLANES
    tm = _pick_tile(n, SAMPLE_ROWS_PER_STEP)
    q_bf, k, k_bf, v, v_bf, gqkv, gz, gates, s, st = _in_proj(x.reshape(n, D_MODEL), w, tm, FOX_HEAD_DIM ** -0.5)

    n_pool, page = k_pool.shape[:2]
    q4 = q_bf.reshape(b, s_new, FOX_HEADS, 1, FOX_HEAD_DIM)
    eye = jnp.eye(FOX_HEADS, dtype=BF16).reshape(1, 1, FOX_HEADS, FOX_HEADS, 1)
    q_bd = (q4 * eye).reshape(b, s_new * FOX_HEADS, FOX_WIDTH)
    pad_keys = lambda a: jnp.pad(a.reshape(b, s_new, FOX_WIDTH).transpose(0, 2, 1),
                                 ((0, 0), (0, 0), (0, LANES - s_new)))
    lf_new_t = jnp.pad(st[ROW_LOGF:ROW_LOGF + FOX_HEADS].reshape(FOX_HEADS, b, s_new).transpose(1, 0, 2),
                       ((0, 0), (0, 0), (0, LANES - s_new)))
    pool_t = lambda a: a.transpose(0, 2, 3, 1).reshape(n_pool, FOX_WIDTH, page)
    o_a = _fox_sample(page_table, q_bd, pool_t(k_pool), pool_t(v_pool), lf_pool.transpose(0, 2, 1),
                      pad_keys(k_bf), pad_keys(v_bf), lf_new_t, pages_per_step=_pick_tile(page_table.shape[1], PAGES_PER_STEP))

    chunk = SUBLANES
    pad_tok = lambda a: jnp.pad(a.reshape(b, s_new, -1), ((0, 0), (0, chunk - s_new), (0, 0)))
    xin = jnp.concatenate([jnp.pad(conv_buf, ((0, 0), (SUBLANES - keep, 0), (0, 0))), pad_tok(gqkv)], axis=1)
    seq_per_step = _pick_tile(b, LANES // chunk)
    o_b, ssm = _gdn_sample(xin.reshape(b * (SUBLANES + chunk), GDN_CONV_CH), pad_tok(s).reshape(b * chunk, LANES),
                           pad_tok(gz).reshape(b * chunk, GDN_V_WIDTH), w["conv_w"], ssm_state, w["gnorm"],
                           b, seq_per_step, chunk)
    o_b = o_b.reshape(b, chunk, GDN_V_WIDTH)[:, :s_new].reshape(n, GDN_V_WIDTH)

    y = _post(x.reshape(n, D_MODEL), o_a.reshape(n, FOX_WIDTH), o_b, gates, ple.reshape(n, PLE_DIM), w, tm)
    states = (k.reshape(b, s_new, FOX_HEADS, FOX_HEAD_DIM), v.reshape(b, s_new, FOX_HEADS, FOX_HEAD_DIM),
              s[:, ROW_LOGF:ROW_LOGF + FOX_HEADS].reshape(b, s_new, FOX_HEADS),
              gqkv.reshape(b, s_new, GDN_CONV_CH)[:, s_new - keep:], ssm)
    return y.reshape(b, s_new, D_MODEL), states


def kernel(x_prompt, x_sample, p_prompt, p_sample, cache_k, cache_v, cache_logf, state_conv, state_ssm, page_table,
           norm_mix_g, w_in, fox_f_bias, fox_q_norm_g, fox_k_norm_g, gdn_conv_w, gdn_a_log, gdn_dt_bias,
           gdn_out_norm_g, w_branch_a, w_branch_b, w_out, norm_mlp_g, w_up, w_down, norm_ple_g, w_ple_gate, w_ple):
    depth = w_in.shape[0]
    y_prompt, y_sample = x_prompt, x_sample
    prompt_states, sample_states = [], []
    for l in range(depth):
        w = _prepare_weights(l, norm_mix_g, w_in, fox_f_bias, fox_q_norm_g, fox_k_norm_g, gdn_conv_w, gdn_a_log,
                             gdn_dt_bias, gdn_out_norm_g, w_branch_a, w_branch_b, w_out, norm_mlp_g, w_up, w_down,
                             norm_ple_g, w_ple_gate, w_ple)
        y_prompt, st_p = _prompt_layer(y_prompt, p_prompt[l], w)
        y_sample, st_s = _sample_layer(y_sample, p_sample[l], w, cache_k[l], cache_v[l], cache_logf[l],
                                       state_conv[l], state_ssm[l], page_table)
        prompt_states.append(st_p)
        sample_states.append(st_s)
    stack = lambda states, i: jnp.stack([st[i] for st in states])
    return ((y_prompt, y_sample) + tuple(stack(prompt_states, i) for i in range(5))
            + tuple(stack(sample_states, i) for i in range(5)))
```
